```python
import jax, jax.numpy as jnp
from jax import lax
import numpy as np

D_MODEL = 2048
BATCH = 4
SEQ = 8192
DEPTH = 1

PLE_DIM = 256
M_HEADS = 4
M_QK_DIM = 128
M_V_DIM = 256
M_CONV = 4
M_CHUNK = 64
F_BIAS_LO = 3.0
F_BIAS_HI = 6.0
A_HEADS = 8
A_HEAD_DIM = 128
MOBA_BLOCK = 256
MOBA_TOPK = 3
A_QCHUNK = 16
ROPE_THETA = 500000.0
ROPE_DIM = A_HEAD_DIM // 4
D_FF = 4 * D_MODEL
EPS = 1e-6

M_QK_W = M_HEADS * M_QK_DIM
M_V_W = M_HEADS * M_V_DIM
A_W = A_HEADS * A_HEAD_DIM
SPLITS = (M_QK_W, M_QK_W, M_V_W, M_V_W, M_HEADS, M_HEADS, A_W, A_W, A_W, D_MODEL, D_MODEL)
IN_WIDTH = M_QK_W * 2 + M_V_W * 2 + M_HEADS * 2 + A_W * 3 + D_MODEL * 2

kernel_name = 'hybrid_mlstm_moba_block'


def rmsnorm(x, g):
    x32 = x.astype(jnp.float32)
    y = x32 * lax.rsqrt(jnp.mean(x32 * x32, axis=-1, keepdims=True) + EPS)
    return (y * g.astype(jnp.float32)).astype(x.dtype)


def to_heads(t, n):
    b, s, _ = t.shape
    return t.reshape(b, s, n, -1).transpose(0, 2, 1, 3)


def causal_dwconv(x, w, b):
    y = lax.conv_general_dilated(
        x, w[:, None, :].astype(x.dtype), window_strides=(1,), padding=[(M_CONV - 1, 0)],
        dimension_numbers=('NWC', 'WIO', 'NWC'), feature_group_count=x.shape[-1])
    return y + b.astype(x.dtype)


def partial_rope(x, positions):
    half = ROPE_DIM // 2
    inv_freq = ROPE_THETA ** (-jnp.arange(half, dtype=jnp.float32) * 2.0 / ROPE_DIM)
    ang = positions.astype(jnp.float32)[:, None, :, None] * inv_freq
    cos, sin = jnp.cos(ang), jnp.sin(ang)
    xr = x[..., :ROPE_DIM].astype(jnp.float32)
    x1, x2 = xr[..., :half], xr[..., half:]
    rot = jnp.concatenate([x1 * cos - x2 * sin, x2 * cos + x1 * sin], axis=-1).astype(x.dtype)
    return jnp.concatenate([rot, x[..., ROPE_DIM:]], axis=-1)


def mlstm_chunkwise(q, k, v, i_pre, f_pre):
    B, H, S, dk = q.shape
    dv = v.shape[-1]
    L = M_CHUNK
    N = S // L
    qc = q.reshape(B, H, N, L, dk)
    kc = k.reshape(B, H, N, L, dk)
    vc = v.reshape(B, H, N, L, dv)
    ig = i_pre.reshape(B, H, N, L)
    bcum = jnp.cumsum(jax.nn.log_sigmoid(f_pre).reshape(B, H, N, L), axis=-1)

    def step(carry, xs):
        C, n, m = carry
        q_, k_, v_, ig_, b_ = xs
        qC = jnp.einsum('bhld,bhdv->bhlv', q_, C)
        qn = jnp.einsum('bhld,bhd->bhl', q_, n)
        g = b_[..., -1]
        a = g[..., None] - b_ + ig_
        m_new = jnp.maximum(g + m, jnp.max(a, axis=-1))
        w = jnp.exp(a - m_new[..., None])
        decay = jnp.exp(g + m - m_new)
        C = decay[..., None, None] * C + jnp.einsum('bhl,bhld,bhlv->bhdv', w, k_, v_)
        n = decay[..., None] * n + jnp.einsum('bhl,bhld->bhd', w, k_)
        return (C, n, m_new), (qC, qn, m)

    xs = (jnp.moveaxis(qc, 2, 0), jnp.moveaxis(kc, 2, 0), jnp.moveaxis(vc, 2, 0),
          jnp.moveaxis(ig, 2, 0), jnp.moveaxis(bcum, 2, 0))
    init = (jnp.zeros((B, H, dk, dv), jnp.float32), jnp.zeros((B, H, dk), jnp.float32),
            jnp.zeros((B, H), jnp.float32))
    _, (qC, qn, m_prev) = lax.scan(step, init, xs)
    qC = jnp.moveaxis(qC, 0, 2)
    qn = jnp.moveaxis(qn, 0, 2)
    m_prev = jnp.moveaxis(m_prev, 0, 2)

    causal = jnp.tril(jnp.ones((L, L), dtype=bool))
    dmat = jnp.where(causal, bcum[..., :, None] - bcum[..., None, :] + ig[..., None, :], -jnp.inf)
    inter = bcum + m_prev[..., None]
    m_t = jnp.maximum(inter, jnp.max(dmat, axis=-1))
    s = jnp.einsum('bhntd,bhnsd->bhnts', qc, kc) * jnp.exp(dmat - m_t[..., None])
    e_inter = jnp.exp(inter - m_t)
    num = e_inter[..., None] * qC + jnp.einsum('bhnts,bhnsv->bhntv', s, vc)
    den = e_inter * qn + jnp.sum(s, axis=-1)
    h = num / jnp.maximum(jnp.abs(den), jnp.exp(-m_t))[..., None]
    return h.reshape(B, H, S, dv)


def head_rmsnorm(h, g):
    B, H, S, dv = h.shape
    y = h * lax.rsqrt(jnp.mean(h * h, axis=-1, keepdims=True) + EPS)
    y = y * g.astype(jnp.float32).reshape(H, dv)[None, :, None, :]
    return y.transpose(0, 2, 1, 3).reshape(B, S, H * dv)


def moba_attention(q, k, v):
    B, H, S, Dh = q.shape
    BLK = MOBA_BLOCK
    NB = -(-S // BLK)
    S_pad = NB * BLK
    pad = ((0, 0), (0, 0), (0, S_pad - S), (0, 0))
    kb = jnp.pad(k, pad).reshape(B, H, NB, BLK, Dh)
    vb = jnp.pad(v, pad).reshape(B, H, NB, BLK, Dh)
    k_mean = jnp.mean(kb.astype(jnp.float32), axis=3)
    scores = jnp.einsum('bhsd,bhnd->bhsn', q.astype(jnp.float32), k_mean)
    q_blk = jnp.arange(S) // BLK
    fully_past = jnp.arange(NB)[None, :] < q_blk[:, None]
    scores = jnp.where(fully_past, scores, -jnp.inf)
    K_SEL = min(MOBA_TOPK, NB)
    _, idx = lax.top_k(scores, K_SEL)

    QC = A_QCHUNK
    Nq = S // QC
    q_ch = jnp.moveaxis(q.reshape(B, H, Nq, QC, Dh), 2, 0)
    idx_ch = jnp.moveaxis(idx.reshape(B, H, Nq, QC, K_SEL), 2, 0)
    starts = jnp.arange(Nq, dtype=jnp.int32) * QC
    bi = jnp.arange(B)[:, None, None, None]
    hi = jnp.arange(H)[None, :, None, None]
    scale = Dh ** -0.5

    def one_chunk(args):
        q_c, idx_c, start = args
        c = start // BLK
        k_sel = kb[bi, hi, idx_c]
        v_sel = vb[bi, hi, idx_c]
        k_own = lax.dynamic_index_in_dim(kb, c, axis=2, keepdims=False)
        v_own = lax.dynamic_index_in_dim(vb, c, axis=2, keepdims=False)
        lp = jnp.einsum('bhqd,bhqjld->bhqjl', q_c, k_sel).astype(jnp.float32) * scale
        lp = jnp.where((jnp.arange(K_SEL) < c)[:, None], lp, -jnp.inf)
        lo = jnp.einsum('bhqd,bhld->bhql', q_c, k_own).astype(jnp.float32) * scale
        q_pos = start + jnp.arange(QC)
        k_pos = c * BLK + jnp.arange(BLK)
        lo = jnp.where(k_pos[None, :] <= q_pos[:, None], lo, -jnp.inf)
        logits = jnp.concatenate([lp.reshape(B, H, QC, K_SEL * BLK), lo], axis=-1)
        w = jax.nn.softmax(logits, axis=-1).astype(v.dtype)
        wp = w[..., :K_SEL * BLK].reshape(B, H, QC, K_SEL, BLK)
        wo = w[..., K_SEL * BLK:]
        return (jnp.einsum('bhqjl,bhqjld->bhqd', wp, v_sel)
                + jnp.einsum('bhql,bhld->bhqd', wo, v_own))

    out = lax.map(one_chunk, (q_ch, idx_ch, starts))
    return jnp.moveaxis(out, 0, 2).reshape(B, H, S, Dh)


def setup_inputs(seed: int = 0) -> dict:
    key = jax.random.key(seed)
    ks = jax.random.split(key, 24)

    def nrm(k, shape, scale):
        return jax.random.normal(k, shape, jnp.float32) * scale

    def gain(k, shape):
        return 1.0 + 0.02 * jax.random.normal(k, shape, jnp.float32)

    f_bias = jnp.linspace(F_BIAS_LO, F_BIAS_HI, M_HEADS, dtype=jnp.float32)
    b_if = jnp.stack([0.1 * jax.random.normal(ks[4], (DEPTH, M_HEADS), jnp.float32),
                      f_bias[None, :] + 0.1 * jax.random.normal(ks[5], (DEPTH, M_HEADS), jnp.float32)],
                     axis=1)
    return {
        'x': nrm(ks[0], (BATCH, SEQ, D_MODEL), 1.0),
        'p': nrm(ks[1], (DEPTH, BATCH, SEQ, PLE_DIM), 1.0),
        'positions': jnp.broadcast_to(jnp.arange(SEQ, dtype=jnp.int32), (BATCH, SEQ)),
        'attn_norm': gain(ks[2], (DEPTH, D_MODEL)),
        'w_in': nrm(ks[3], (DEPTH, D_MODEL, IN_WIDTH), D_MODEL ** -0.5),
        'b_if': b_if,
        'conv_w': nrm(ks[6], (DEPTH, M_CONV, 2 * M_QK_W), M_CONV ** -0.5),
        'conv_b': nrm(ks[7], (DEPTH, 2 * M_QK_W), 0.01),
        'm_out_norm': gain(ks[8], (DEPTH, M_V_W)),
        'w_up_m': nrm(ks[9], (DEPTH, M_V_W, D_MODEL), M_V_W ** -0.5),
        'w_up_a': nrm(ks[10], (DEPTH, A_W, D_MODEL), A_W ** -0.5),
        'w_out': nrm(ks[11], (DEPTH, D_MODEL, D_MODEL), D_MODEL ** -0.5),
        'mlp_norm': gain(ks[12], (DEPTH, D_MODEL)),
        'w_ff1': nrm(ks[13], (DEPTH, D_MODEL, D_FF), D_MODEL ** -0.5),
        'w_ff2': nrm(ks[14], (DEPTH, D_FF, D_MODEL), D_FF ** -0.5),
        'ple_norm': gain(ks[15], (DEPTH, D_MODEL)),
        'w_ple_gate': nrm(ks[16], (DEPTH, D_MODEL, D_MODEL), D_MODEL ** -0.5),
        'w_ple_proj': nrm(ks[17], (DEPTH, PLE_DIM, D_MODEL), PLE_DIM ** -0.5),
        'final_norm': gain(ks[18], (D_MODEL,)),
    }


def reference(x, p, positions, attn_norm, w_in, b_if, conv_w, conv_b, m_out_norm, w_up_m, w_up_a,
              w_out, mlp_norm, w_ff1, w_ff2, ple_norm, w_ple_gate, w_ple_proj, final_norm):
    offs = np.cumsum(np.array(SPLITS))[:-1].tolist()
    for i in range(DEPTH):
        h = rmsnorm(x, attn_norm[i])
        z = h @ w_in[i]
        mq, mk, mv, mo, mi, mf, aq, ak, av, gm, ga = jnp.split(z, offs, axis=-1)

        qk = jax.nn.silu(causal_dwconv(jnp.concatenate([mq, mk], axis=-1), conv_w[i], conv_b[i]))
        mq, mk = qk[..., :M_QK_W], qk[..., M_QK_W:]
        i_pre = (mi.astype(jnp.float32) + b_if[i, 0].astype(jnp.float32)).transpose(0, 2, 1)
        f_pre = (mf.astype(jnp.float32) + b_if[i, 1].astype(jnp.float32)).transpose(0, 2, 1)
        hm = mlstm_chunkwise(to_heads(mq, M_HEADS).astype(jnp.float32),
                             to_heads(mk, M_HEADS).astype(jnp.float32) * (M_QK_DIM ** -0.5),
                             to_heads(mv, M_HEADS).astype(jnp.float32), i_pre, f_pre)
        hm = head_rmsnorm(hm, m_out_norm[i]).astype(x.dtype) * jax.nn.sigmoid(mo)

        qa = partial_rope(to_heads(aq, A_HEADS), positions)
        ka = partial_rope(to_heads(ak, A_HEADS), positions)
        va = to_heads(av, A_HEADS)
        ha = moba_attention(qa, ka, va)
        ha = ha.transpose(0, 2, 1, 3).reshape(x.shape[0], x.shape[1], A_W)

        merged = (jax.nn.sigmoid(gm) * (hm @ w_up_m[i])
                  + jax.nn.sigmoid(ga) * (ha @ w_up_a[i]))
        x = x + merged @ w_out[i]

        h = rmsnorm(x, mlp_norm[i])
        x = x + jnp.square(jax.nn.relu(h @ w_ff1[i])) @ w_ff2[i]

        h = rmsnorm(x, ple_norm[i])
        x = x + jax.nn.sigmoid(h @ w_ple_gate[i]) * (p[i] @ w_ple_proj[i])
    return rmsnorm(x, final_norm)
```

```python
import functools

import jax
import jax.numpy as jnp
from jax import lax
from jax.experimental import pallas as pl
from jax.experimental.pallas import tpu as pltpu

D_MODEL = 2048
PLE_DIM = 256
M_HEADS = 4
M_QK_DIM = 128
M_V_DIM = 256
M_CONV = 4
A_HEADS = 8
A_HEAD_DIM = 128
MOBA_BLOCK = 256
MOBA_TOPK = 3
ROPE_THETA = 500000.0
ROPE_DIM = A_HEAD_DIM // 4
D_FF = 4 * D_MODEL
EPS = 1e-6

M_QK_W = M_HEADS * M_QK_DIM
M_V_W = M_HEADS * M_V_DIM
A_W = A_HEADS * A_HEAD_DIM

LANES = 128
SUBLANES = 8
GATE_W = LANES
ZM_W = 2 * M_QK_W + 2 * M_V_W + GATE_W
VMEM_LIMIT = 56 * 1024 * 1024

F32 = jnp.float32
BF16 = jnp.bfloat16
NEG_INF = float("-inf")


def _cparams(*sem):
    return pltpu.CompilerParams(dimension_semantics=sem, vmem_limit_bytes=VMEM_LIMIT)


def _rmsnorm(x, g):
    ms = jnp.mean(x * x, axis=-1, keepdims=True)
    return (x * lax.rsqrt(ms + EPS)) * g


def _sigmoid(x):
    return 1.0 / (1.0 + jnp.exp(-x))


def _dot(a, b):
    return jnp.dot(a, b, preferred_element_type=F32)


def _dot_nt(a, b):
    return lax.dot_general(a, b, (((1,), (1,)), ((), ())), preferred_element_type=F32)


def _inproj_m_kernel(x_ref, g_ref, w_ref, o_ref, h_ref):
    @pl.when(pl.program_id(1) == 0)
    def _():
        h_ref[...] = _rmsnorm(x_ref[...], g_ref[...]).astype(BF16)

    o_ref[...] = _dot(h_ref[...], w_ref[...])


def _inproj_m(x, g, w, *, tm, tn):
    t, d = x.shape
    n = w.shape[1]
    tm = min(tm, t)
    return pl.pallas_call(
        _inproj_m_kernel,
        out_shape=(jax.ShapeDtypeStruct((t, n), F32), jax.ShapeDtypeStruct((t, d), BF16)),
        grid=(t // tm, n // tn),
        in_specs=[
            pl.BlockSpec((tm, d), lambda i, j: (i, 0)),
            pl.BlockSpec((1, d), lambda i, j: (0, 0)),
            pl.BlockSpec((d, tn), lambda i, j: (0, j)),
        ],
        out_specs=(pl.BlockSpec((tm, tn), lambda i, j: (i, j)),
                   pl.BlockSpec((tm, d), lambda i, j: (i, 0))),
        compiler_params=_cparams("arbitrary", "arbitrary"),
        name="inproj_m",
    )(x, g, w)


def _inproj_a_kernel(h_ref, w_ref, pos_ref, freq_ref, o_ref, cos_ref, sin_ref):
    j = pl.program_id(1)

    @pl.when(j == 0)
    def _():
        ang = pos_ref[...].astype(F32) * freq_ref[...]
        cos_ref[...] = jnp.cos(ang)
        sin_ref[...] = jnp.sin(ang)

    z = _dot(h_ref[...], w_ref[...])

    @pl.when(j < 2)
    def _():
        c = cos_ref[...]
        s = sin_ref[...]
        lane = lax.broadcasted_iota(jnp.int32, c.shape, 1)
        half = ROPE_DIM // 2
        for hd in range(A_HEADS):
            zh = z[:, hd * A_HEAD_DIM:(hd + 1) * A_HEAD_DIM]
            partner = jnp.where(lane < half,
                                pltpu.roll(zh, A_HEAD_DIM - half, 1),
                                pltpu.roll(zh, half, 1))
            rot = jnp.where(lane < ROPE_DIM, zh * c + partner * s, zh)
            o_ref[:, hd * A_HEAD_DIM:(hd + 1) * A_HEAD_DIM] = rot.astype(o_ref.dtype)

    @pl.when(j == 2)
    def _():
        o_ref[...] = z.astype(o_ref.dtype)


def _inproj_a(h, w, pos, freq, *, tm):
    t, d = h.shape
    tm = min(tm, t)
    return pl.pallas_call(
        _inproj_a_kernel,
        out_shape=jax.ShapeDtypeStruct((t, 3 * A_W), BF16),
        grid=(t // tm, 3),
        in_specs=[
            pl.BlockSpec((tm, d), lambda i, j: (i, 0)),
            pl.BlockSpec((d, A_W), lambda i, j: (0, j)),
            pl.BlockSpec((tm, 1), lambda i, j: (i, 0)),
            pl.BlockSpec((1, A_HEAD_DIM), lambda i, j: (0, 0)),
        ],
        out_specs=pl.BlockSpec((tm, A_W), lambda i, j: (i, j)),
        scratch_shapes=[
            pltpu.VMEM((tm, A_HEAD_DIM), F32),
            pltpu.VMEM((tm, A_HEAD_DIM), F32),
        ],
        compiler_params=_cparams("arbitrary", "arbitrary"),
        name="inproj_a",
    )(h, w, pos, freq)


def _mlstm_kernel(qk_ref, v_ref, o_ref, gate_ref, cw_ref, cb_ref, gb_ref, gn_ref, out_ref,
                  buf_ref, c_ref, n_ref, m_ref, *, lc):
    ci = pl.program_id(1)
    pad = SUBLANES

    @pl.when(ci == 0)
    def _():
        buf_ref[0:pad, :] = jnp.zeros((pad, 2 * M_QK_W), F32)
        c_ref[...] = jnp.zeros(c_ref.shape, F32)
        n_ref[...] = jnp.zeros(n_ref.shape, F32)
        m_ref[...] = jnp.zeros(m_ref.shape, F32)

    buf_ref[pad:pad + lc, :] = qk_ref[...]
    y = cb_ref[...] + cw_ref[M_CONV - 1:M_CONV, :] * buf_ref[pad:pad + lc, :]
    for j in range(1, M_CONV):
        y = y + cw_ref[M_CONV - 1 - j:M_CONV - j, :] * buf_ref[pad - j:pad - j + lc, :]
    buf_ref[0:pad, :] = buf_ref[lc:lc + pad, :]
    qk = y * _sigmoid(y)

    gates = gate_ref[...] + gb_ref[...]
    lf = jnp.minimum(gates, 0.0) - jnp.log1p(jnp.exp(-jnp.abs(gates)))
    row = lax.broadcasted_iota(jnp.int32, (lc, lc), 0)
    col = lax.broadcasted_iota(jnp.int32, (lc, lc), 1)
    causal = row >= col
    tri = causal.astype(F32)
    bcum = jnp.dot(tri, lf, preferred_element_type=F32, precision=lax.Precision.HIGHEST)
    gates_t = gates.T
    bcum_t = bcum.T

    for hd in range(M_HEADS):
        q = qk[:, hd * M_QK_DIM:(hd + 1) * M_QK_DIM]
        k = qk[:, M_QK_W + hd * M_QK_DIM:M_QK_W + (hd + 1) * M_QK_DIM] * (M_QK_DIM ** -0.5)
        v = v_ref[:, hd * M_V_DIM:(hd + 1) * M_V_DIM].astype(BF16)
        q_bf = q.astype(BF16)
        k_bf = k.astype(BF16)
        ig_col = gates[:, hd:hd + 1]
        ig_row = gates_t[hd:hd + 1, :]
        b_col = bcum[:, M_HEADS + hd:M_HEADS + hd + 1]
        b_row = bcum_t[M_HEADS + hd:M_HEADS + hd + 1, :]
        m_prev = m_ref[hd][:, 0:1]
        c_prev = c_ref[hd]
        n_prev = n_ref[hd]

        dmat = jnp.where(causal, b_col - b_row + ig_row, NEG_INF)
        inter = b_col + m_prev
        m_t = jnp.maximum(inter, jnp.max(dmat, axis=-1, keepdims=True))
        s = _dot_nt(q_bf, k_bf) * jnp.exp(dmat - m_t)
        e_inter = jnp.exp(inter - m_t)
        q_c = _dot(q_bf, c_prev.astype(BF16))
        q_n = jnp.sum(q * n_prev, axis=-1, keepdims=True)
        num = e_inter * q_c + _dot(s.astype(BF16), v)
        den = e_inter * q_n + jnp.sum(s, axis=-1, keepdims=True)
        h = num / jnp.maximum(jnp.abs(den), jnp.exp(-m_t))

        hn = h * lax.rsqrt(jnp.mean(h * h, axis=-1, keepdims=True) + EPS)
        hn = hn * gn_ref[:, hd * M_V_DIM:(hd + 1) * M_V_DIM]
        og = _sigmoid(o_ref[:, hd * M_V_DIM:(hd + 1) * M_V_DIM])
        out_ref[:, hd * M_V_DIM:(hd + 1) * M_V_DIM] = (hn * og).astype(out_ref.dtype)

        g_last = b_col[lc - 1:lc, :]
        a_col = g_last - b_col + ig_col
        m_new = jnp.maximum(g_last + m_prev, jnp.max(a_col, axis=0, keepdims=True))
        w_col = jnp.exp(a_col - m_new)
        decay = jnp.exp(g_last + m_prev - m_new)
        kw = k * w_col
        c_ref[hd] = decay * c_prev + _dot(kw.T.astype(BF16), v)
        n_ref[hd] = decay * n_prev + jnp.sum(kw, axis=0, keepdims=True)
        m_ref[hd] = jnp.broadcast_to(m_new, (1, LANES))


def _mlstm(zm, conv_w, conv_b, gate_b, out_norm, *, batch, seq, lc):
    t = zm.shape[0]
    nc = seq // lc
    qkw = 2 * M_QK_W
    return pl.pallas_call(
        functools.partial(_mlstm_kernel, lc=lc),
        out_shape=jax.ShapeDtypeStruct((t, M_V_W), BF16),
        grid=(batch, nc),
        in_specs=[
            pl.BlockSpec((lc, qkw), lambda b, c: (b * nc + c, 0)),
            pl.BlockSpec((lc, M_V_W), lambda b, c: (b * nc + c, qkw // M_V_W)),
            pl.BlockSpec((lc, M_V_W), lambda b, c: (b * nc + c, qkw // M_V_W + 1)),
            pl.BlockSpec((lc, GATE_W), lambda b, c: (b * nc + c, (qkw + 2 * M_V_W) // GATE_W)),
            pl.BlockSpec((M_CONV, qkw), lambda b, c: (0, 0)),
            pl.BlockSpec((1, qkw), lambda b, c: (0, 0)),
            pl.BlockSpec((1, GATE_W), lambda b, c: (0, 0)),
            pl.BlockSpec((1, M_V_W), lambda b, c: (0, 0)),
        ],
        out_specs=pl.BlockSpec((lc, M_V_W), lambda b, c: (b * nc + c, 0)),
        scratch_shapes=[
            pltpu.VMEM((lc + SUBLANES, qkw), F32),
            pltpu.VMEM((M_HEADS, M_QK_DIM, M_V_DIM), F32),
            pltpu.VMEM((M_HEADS, 1, M_QK_DIM), F32),
            pltpu.VMEM((M_HEADS, 1, LANES), F32),
        ],
        compiler_params=_cparams("arbitrary", "arbitrary"),
        name="mlstm",
    )(zm, zm, zm, zm, conv_w, conv_b, gate_b, out_norm)


def _moba_kernel(q_ref, k_ref, v_ref, o_ref, vt_ref, kmean_ref, sel_ref, m_ref, l_ref, acc_ref,
                 *, nblk):
    qi = pl.program_id(2)
    blk = MOBA_BLOCK
    scale = A_HEAD_DIM ** -0.5

    @pl.when(qi == 0)
    def _():
        def prep(j, carry):
            r0 = pl.multiple_of(j * blk, blk)
            vt_ref[j] = v_ref[pl.ds(r0, blk), :].astype(F32).T.astype(BF16)
            kj = k_ref[pl.ds(r0, blk), :].astype(F32)
            kmean_ref[pl.ds(j, 1), :] = jnp.mean(kj, axis=0, keepdims=True)
            return carry
        lax.fori_loop(0, nblk, prep, 0)

    q = q_ref[...]

    sc = lax.dot_general(kmean_ref[...], q.astype(F32), (((1,), (1,)), ((), ())),
                         preferred_element_type=F32, precision=lax.Precision.HIGHEST)
    blk_id = lax.broadcasted_iota(jnp.int32, sc.shape, 0)
    past = blk_id < qi
    sc = jnp.where(past, sc, NEG_INF)
    rank = jnp.zeros(sc.shape, jnp.int32)
    for jp in range(nblk):
        other = sc[jp:jp + 1, :]
        beats = (other > sc) | ((other == sc) & (blk_id > jp))
        rank = rank + beats.astype(jnp.int32)
    sel_ref[...] = (past & (rank < MOBA_TOPK)).astype(F32)

    r0 = pl.multiple_of(qi * blk, blk)
    s = _dot_nt(k_ref[pl.ds(r0, blk), :], q) * scale
    kpos = lax.broadcasted_iota(jnp.int32, s.shape, 0)
    qpos = lax.broadcasted_iota(jnp.int32, s.shape, 1)
    s = jnp.where(kpos <= qpos, s, NEG_INF)
    m0 = jnp.max(s, axis=0, keepdims=True)
    p = jnp.exp(s - m0)
    m_ref[...] = m0
    l_ref[...] = jnp.sum(p, axis=0, keepdims=True)
    acc_ref[...] = _dot(vt_ref[qi], p.astype(BF16))

    def body(j, carry):
        c0 = pl.multiple_of(j * blk, blk)
        sj = _dot_nt(k_ref[pl.ds(c0, blk), :], q) * scale
        sj = jnp.where(sel_ref[pl.ds(j, 1), :] > 0.0, sj, NEG_INF)
        m_old = m_ref[...]
        m_new = jnp.maximum(m_old, jnp.max(sj, axis=0, keepdims=True))
        alpha = jnp.exp(m_old - m_new)
        pj = jnp.exp(sj - m_new)
        l_ref[...] = alpha * l_ref[...] + jnp.sum(pj, axis=0, keepdims=True)
        acc_ref[...] = alpha * acc_ref[...] + _dot(vt_ref[j], pj.astype(BF16))
        m_ref[...] = m_new
        return carry
    lax.fori_loop(0, qi, body, 0)

    o_ref[...] = (acc_ref[...] / l_ref[...]).T.astype(o_ref.dtype)


def _moba(qkv, *, batch, seq):
    t = qkv.shape[0]
    nblk = seq // MOBA_BLOCK
    hd = A_HEAD_DIM
    return pl.pallas_call(
        functools.partial(_moba_kernel, nblk=nblk),
        out_shape=jax.ShapeDtypeStruct((t, A_W), BF16),
        grid=(batch, A_HEADS, nblk),
        in_specs=[
            pl.BlockSpec((MOBA_BLOCK, hd), lambda b, h, i: (b * nblk + i, h)),
            pl.BlockSpec((seq, hd), lambda b, h, i: (b, A_HEADS + h)),
            pl.BlockSpec((seq, hd), lambda b, h, i: (b, 2 * A_HEADS + h)),
        ],
        out_specs=pl.BlockSpec((MOBA_BLOCK, hd), lambda b, h, i: (b * nblk + i, h)),
        scratch_shapes=[
            pltpu.VMEM((nblk, hd, MOBA_BLOCK), BF16),
            pltpu.VMEM((nblk, hd), F32),
            pltpu.VMEM((nblk, MOBA_BLOCK), F32),
            pltpu.VMEM((1, MOBA_BLOCK), F32),
            pltpu.VMEM((1, MOBA_BLOCK), F32),
            pltpu.VMEM((hd, MOBA_BLOCK), F32),
        ],
        compiler_params=_cparams("arbitrary", "arbitrary", "arbitrary"),
        name="moba",
    )(qkv, qkv, qkv)


def _merge_kernel(h_ref, hm_ref, ha_ref, wgm_ref, wga_ref, wm_ref, wa_ref, o_ref):
    h = h_ref[...]
    gm = _sigmoid(_dot(h, wgm_ref[...]))
    ga = _sigmoid(_dot(h, wga_ref[...]))
    um = _dot(hm_ref[...], wm_ref[...])
    ua = _dot(ha_ref[...], wa_ref[...])
    o_ref[...] = (gm * um + ga * ua).astype(o_ref.dtype)


def _merge(h, hm, ha, wgm, wga, wm, wa, *, tm, tn):
    t, d = h.shape
    tm = min(tm, t)
    return pl.pallas_call(
        _merge_kernel,
        out_shape=jax.ShapeDtypeStruct((t, d), BF16),
        grid=(t // tm, d // tn),
        in_specs=[
            pl.BlockSpec((tm, d), lambda i, j: (i, 0)),
            pl.BlockSpec((tm, M_V_W), lambda i, j: (i, 0)),
            pl.BlockSpec((tm, A_W), lambda i, j: (i, 0)),
            pl.BlockSpec((d, tn), lambda i, j: (0, j)),
            pl.BlockSpec((d, tn), lambda i, j: (0, j)),
            pl.BlockSpec((M_V_W, tn), lambda i, j: (0, j)),
            pl.BlockSpec((A_W, tn), lambda i, j: (0, j)),
        ],
        out_specs=pl.BlockSpec((tm, tn), lambda i, j: (i, j)),
        compiler_params=_cparams("arbitrary", "arbitrary"),
        name="merge",
    )(h, hm, ha, wgm, wga, wm, wa)


def _outproj_kernel(x_ref, a_ref, w_ref, o_ref):
    o_ref[...] = x_ref[...] + _dot(a_ref[...], w_ref[...])


def _outproj(x, a, w, *, tm, tn):
    t, d = x.shape
    k = a.shape[1]
    tm = min(tm, t)
    return pl.pallas_call(
        _outproj_kernel,
        out_shape=jax.ShapeDtypeStruct((t, d), F32),
        grid=(t // tm, d // tn),
        in_specs=[
            pl.BlockSpec((tm, tn), lambda i, j: (i, j)),
            pl.BlockSpec((tm, k), lambda i, j: (i, 0)),
            pl.BlockSpec((k, tn), lambda i, j: (0, j)),
        ],
        out_specs=pl.BlockSpec((tm, tn), lambda i, j: (i, j)),
        compiler_params=_cparams("arbitrary", "arbitrary"),
        name="outproj",
    )(x, a, w)


def _ffn_kernel(x_ref, g_ref, w1_ref, w2_ref, o_ref, h_ref):
    @pl.when(pl.program_id(1) == 0)
    def _():
        h_ref[...] = _rmsnorm(x_ref[...], g_ref[...]).astype(BF16)
        o_ref[...] = x_ref[...]

    u = jnp.maximum(_dot(h_ref[...], w1_ref[...]), 0.0)
    o_ref[...] += _dot((u * u).astype(BF16), w2_ref[...])


def _ffn(x, g, w1, w2, *, tm, tf):
    t, d = x.shape
    dff = w1.shape[1]
    tm = min(tm, t)
    return pl.pallas_call(
        _ffn_kernel,
        out_shape=jax.ShapeDtypeStruct((t, d), F32),
        grid=(t // tm, dff // tf),
        in_specs=[
            pl.BlockSpec((tm, d), lambda i, f: (i, 0)),
            pl.BlockSpec((1, d), lambda i, f: (0, 0)),
            pl.BlockSpec((d, tf), lambda i, f: (0, f)),
            pl.BlockSpec((tf, d), lambda i, f: (f, 0)),
        ],
        out_specs=pl.BlockSpec((tm, d), lambda i, f: (i, 0)),
        scratch_shapes=[pltpu.VMEM((tm, d), BF16)],
        compiler_params=_cparams("arbitrary", "arbitrary"),
        name="ffn",
    )(x, g, w1, w2)


def _ple_kernel(x_ref, p_ref, g_ref, wg_ref, wp_ref, gf_ref, o_ref, *, final):
    x = x_ref[...]
    h = _rmsnorm(x, g_ref[...]).astype(BF16)
    gate = _sigmoid(_dot(h, wg_ref[...]))
    emb = _dot(p_ref[...].astype(BF16), wp_ref[...])
    y = x + gate * emb
    o_ref[...] = _rmsnorm(y, gf_ref[...]) if final else y


def _ple(x, p, g, wg, wp, gf, *, tm, final):
    t, d = x.shape
    tm = min(tm, t)
    return pl.pallas_call(
        functools.partial(_ple_kernel, final=final),
        out_shape=jax.ShapeDtypeStruct((t, d), F32),
        grid=(t // tm,),
        in_specs=[
            pl.BlockSpec((tm, d), lambda i: (i, 0)),
            pl.BlockSpec((tm, PLE_DIM), lambda i: (i, 0)),
            pl.BlockSpec((1, d), lambda i: (0, 0)),
            pl.BlockSpec((d, d), lambda i: (0, 0)),
            pl.BlockSpec((PLE_DIM, d), lambda i: (0, 0)),
            pl.BlockSpec((1, d), lambda i: (0, 0)),
        ],
        out_specs=pl.BlockSpec((tm, d), lambda i: (i, 0)),
        compiler_params=_cparams("arbitrary"),
        name="ple",
    )(x, p, g, wg, wp, gf)


def _layer(x2d, p2d, pos2d, batch, seq, final_norm, final, attn_norm, w_in, b_if, conv_w, conv_b,
           m_out_norm, w_up_m, w_up_a, w_out, mlp_norm, w_ff1, w_ff2, ple_norm, w_ple_gate,
           w_ple_proj):
    d = D_MODEL
    o_if = 2 * M_QK_W + 2 * M_V_W
    o_a = o_if + 2 * M_HEADS
    o_g = o_a + 3 * A_W
    w_gate = jnp.pad(w_in[:, o_if:o_a], ((0, 0), (0, GATE_W - 2 * M_HEADS)))
    w_m = jnp.concatenate([w_in[:, :o_if], w_gate], axis=1).astype(BF16)
    w_a = w_in[:, o_a:o_g].astype(BF16)
    w_gm = w_in[:, o_g:o_g + d].astype(BF16)
    w_ga = w_in[:, o_g + d:].astype(BF16)
    gate_b = jnp.pad(b_if.reshape(1, 2 * M_HEADS), ((0, 0), (0, GATE_W - 2 * M_HEADS)))
    g_attn = attn_norm.reshape(1, d)

    half = ROPE_DIM // 2
    inv_freq = ROPE_THETA ** (-jnp.arange(half, dtype=F32) * 2.0 / ROPE_DIM)
    freq = jnp.concatenate([-inv_freq, inv_freq, jnp.zeros((A_HEAD_DIM - ROPE_DIM,), F32)]).reshape(1, -1)

    zm, h = _inproj_m(x2d, g_attn, w_m, tm=1024, tn=ZM_W // 5)
    qkv = _inproj_a(h, w_a, pos2d, freq, tm=1024)
    hm = _mlstm(zm, conv_w, conv_b.reshape(1, -1), gate_b, m_out_norm.reshape(1, -1),
                batch=batch, seq=seq, lc=256)
    ha = _moba(qkv, batch=batch, seq=seq)
    merged = _merge(h, hm, ha, w_gm, w_ga, w_up_m.astype(BF16), w_up_a.astype(BF16),
                    tm=1024, tn=512)
    x1 = _outproj(x2d, merged, w_out.astype(BF16), tm=1024, tn=1024)
    x2 = _ffn(x1, mlp_norm.reshape(1, d), w_ff1.astype(BF16), w_ff2.astype(BF16), tm=512, tf=512)
    return _ple(x2, p2d, ple_norm.reshape(1, d), w_ple_gate.astype(BF16), w_ple_proj.astype(BF16),
                final_norm.reshape(1, d), tm=512, final=final)


def kernel(x, p, positions, attn_norm, w_in, b_if, conv_w, conv_b, m_out_norm, w_up_m, w_up_a,
           w_out, mlp_norm, w_ff1, w_ff2, ple_norm, w_ple_gate, w_ple_proj, final_norm):
    batch, seq, d = x.shape
    depth = w_in.shape[0]
    t = batch * seq
    x2d = x.reshape(t, d)
    pos2d = positions.reshape(t, 1)
    for i in range(depth):
        x2d = _layer(
            x2d, p[i].reshape(t, PLE_DIM), pos2d, batch, seq, final_norm, i == depth - 1,
            attn_norm[i], w_in[i], b_if[i], conv_w[i], conv_b[i], m_out_norm[i], w_up_m[i],
            w_up_a[i], w_out[i], mlp_norm[i], w_ff1[i], w_ff2[i], ple_norm[i], w_ple_gate[i],
            w_ple_proj[i])
    return x2d.reshape(batch, seq, d)
```

```python
import functools

import jax
import jax.numpy as jnp
from jax import lax
from jax.experimental import pallas as pl
from jax.experimental.pallas import tpu as pltpu

D_MODEL = 2048
PLE_DIM = 256
M_HEADS = 4
M_QK_DIM = 128
M_V_DIM = 256
M_CONV = 4
A_HEADS = 8
A_HEAD_DIM = 128
MOBA_BLOCK = 256
MOBA_TOPK = 3
ROPE_THETA = 500000.0
ROPE_DIM = A_HEAD_DIM // 4
D_FF = 4 * D_MODEL
EPS = 1e-6

M_QK_W = M_HEADS * M_QK_DIM
M_V_W = M_HEADS * M_V_DIM
A_W = A_HEADS * A_HEAD_DIM

LANES = 128
SUBLANES = 8
GATE_W = LANES
ZM_W = 2 * M_QK_W + 2 * M_V_W + GATE_W
VMEM_LIMIT = 56 * 1024 * 1024

F32 = jnp.float32
BF16 = jnp.bfloat16
NEG_INF = float("-inf")


def _cparams(*sem):
    return pltpu.CompilerParams(dimension_semantics=sem, vmem_limit_bytes=VMEM_LIMIT)


def _rmsnorm(x, g):
    ms = jnp.mean(x * x, axis=-1, keepdims=True)
    return (x * lax.rsqrt(ms + EPS)) * g


def _sigmoid(x):
    return 1.0 / (1.0 + jnp.exp(-x))


def _dot(a, b):
    return jnp.dot(a, b, preferred_element_type=F32)


def _dot_nt(a, b):
    return lax.dot_general(a, b, (((1,), (1,)), ((), ())), preferred_element_type=F32)


def _inproj_m_kernel(x_ref, g_ref, w_ref, o_ref, h_ref):
    @pl.when(pl.program_id(1) == 0)
    def _():
        h_ref[...] = _rmsnorm(x_ref[...], g_ref[...]).astype(BF16)

    o_ref[...] = _dot(h_ref[...], w_ref[...])


def _inproj_m(x, g, w, *, tm, tn):
    t, d = x.shape
    n = w.shape[1]
    tm = min(tm, t)
    return pl.pallas_call(
        _inproj_m_kernel,
        out_shape=(jax.ShapeDtypeStruct((t, n), F32), jax.ShapeDtypeStruct((t, d), BF16)),
        grid=(t // tm, n // tn),
        in_specs=[
            pl.BlockSpec((tm, d), lambda i, j: (i, 0)),
            pl.BlockSpec((1, d), lambda i, j: (0, 0)),
            pl.BlockSpec((d, tn), lambda i, j: (0, j)),
        ],
        out_specs=(pl.BlockSpec((tm, tn), lambda i, j: (i, j)),
                   pl.BlockSpec((tm, d), lambda i, j: (i, 0))),
        compiler_params=_cparams("arbitrary", "arbitrary"),
        name="inproj_m",
    )(x, g, w)


def _inproj_a_kernel(h_ref, w_ref, pos_ref, freq_ref, o_ref, cos_ref, sin_ref):
    j = pl.program_id(1)

    @pl.when(j == 0)
    def _():
        ang = pos_ref[...].astype(F32) * freq_ref[...]
        cos_ref[...] = jnp.cos(ang)
        sin_ref[...] = jnp.sin(ang)

    z = _dot(h_ref[...], w_ref[...])

    @pl.when(j < 2)
    def _():
        c = cos_ref[...]
        s = sin_ref[...]
        lane = lax.broadcasted_iota(jnp.int32, c.shape, 1)
        half = ROPE_DIM // 2
        for hd in range(A_HEADS):
            zh = z[:, hd * A_HEAD_DIM:(hd + 1) * A_HEAD_DIM]
            partner = jnp.where(lane < half,
                                pltpu.roll(zh, A_HEAD_DIM - half, 1),
                                pltpu.roll(zh, half, 1))
            rot = jnp.where(lane < ROPE_DIM, zh * c + partner * s, zh)
            o_ref[:, hd * A_HEAD_DIM:(hd + 1) * A_HEAD_DIM] = rot.astype(o_ref.dtype)

    @pl.when(j == 2)
    def _():
        o_ref[...] = z.astype(o_ref.dtype)


def _inproj_a(h, w, pos, freq, *, tm):
    t, d = h.shape
    tm = min(tm, t)
    return pl.pallas_call(
        _inproj_a_kernel,
        out_shape=jax.ShapeDtypeStruct((t, 3 * A_W), BF16),
        grid=(t // tm, 3),
        in_specs=[
            pl.BlockSpec((tm, d), lambda i, j: (i, 0)),
            pl.BlockSpec((d, A_W), lambda i, j: (0, j)),
            pl.BlockSpec((tm, 1), lambda i, j: (i, 0)),
            pl.BlockSpec((1, A_HEAD_DIM), lambda i, j: (0, 0)),
        ],
        out_specs=pl.BlockSpec((tm, A_W), lambda i, j: (i, j)),
        scratch_shapes=[
            pltpu.VMEM((tm, A_HEAD_DIM), F32),
            pltpu.VMEM((tm, A_HEAD_DIM), F32),
        ],
        compiler_params=_cparams("arbitrary", "arbitrary"),
        name="inproj_a",
    )(h, w, pos, freq)


def _mlstm_kernel(qk_ref, v_ref, o_ref, gate_ref, cw_ref, cb_ref, gb_ref, gn_ref, out_ref,
                  buf_ref, c_ref, n_ref, m_ref, *, lc):
    ci = pl.program_id(1)
    pad = SUBLANES

    @pl.when(ci == 0)
    def _():
        buf_ref[0:pad, :] = jnp.zeros((pad, 2 * M_QK_W), F32)
        c_ref[...] = jnp.zeros(c_ref.shape, F32)
        n_ref[...] = jnp.zeros(n_ref.shape, F32)
        m_ref[...] = jnp.zeros(m_ref.shape, F32)

    buf_ref[pad:pad + lc, :] = qk_ref[...]
    y = cb_ref[...] + cw_ref[M_CONV - 1:M_CONV, :] * buf_ref[pad:pad + lc, :]
    for j in range(1, M_CONV):
        y = y + cw_ref[M_CONV - 1 - j:M_CONV - j, :] * buf_ref[pad - j:pad - j + lc, :]
    buf_ref[0:pad, :] = buf_ref[lc:lc + pad, :]
    qk = y * _sigmoid(y)

    gates = gate_ref[...] + gb_ref[...]
    lf = jnp.minimum(gates, 0.0) - jnp.log1p(jnp.exp(-jnp.abs(gates)))
    row = lax.broadcasted_iota(jnp.int32, (lc, lc), 0)
    col = lax.broadcasted_iota(jnp.int32, (lc, lc), 1)
    causal = row >= col
    tri = causal.astype(F32)
    bcum = jnp.dot(tri, lf, preferred_element_type=F32, precision=lax.Precision.HIGHEST)
    gates_t = gates.T
    bcum_t = bcum.T

    for hd in range(M_HEADS):
        q = qk[:, hd * M_QK_DIM:(hd + 1) * M_QK_DIM]
        k = qk[:, M_QK_W + hd * M_QK_DIM:M_QK_W + (hd + 1) * M_QK_DIM] * (M_QK_DIM ** -0.5)
        v = v_ref[:, hd * M_V_DIM:(hd + 1) * M_V_DIM].astype(BF16)
        q_bf = q.astype(BF16)
        k_bf = k.astype(BF16)
        ig_col = gates[:, hd:hd + 1]
        ig_row = gates_t[hd:hd + 1, :]
        b_col = bcum[:, M_HEADS + hd:M_HEADS + hd + 1]
        b_row = bcum_t[M_HEADS + hd:M_HEADS + hd + 1, :]
        m_prev = m_ref[hd][:, 0:1]
        c_prev = c_ref[hd]
        n_prev = n_ref[hd]

        dmat = jnp.where(causal, b_col - b_row + ig_row, NEG_INF)
        inter = b_col + m_prev
        m_t = jnp.maximum(inter, jnp.max(dmat, axis=-1, keepdims=True))
        s = _dot_nt(q_bf, k_bf) * jnp.exp(dmat - m_t)
        e_inter = jnp.exp(inter - m_t)
        q_c = _dot(q_bf, c_prev.astype(BF16))
        q_n = jnp.sum(q * n_prev, axis=-1, keepdims=True)
        num = e_inter * q_c + _dot(s.astype(BF16), v)
        den = e_inter * q_n + jnp.sum(s, axis=-1, keepdims=True)
        h = num / jnp.maximum(jnp.abs(den), jnp.exp(-m_t))

        hn = h * lax.rsqrt(jnp.mean(h * h, axis=-1, keepdims=True) + EPS)
        hn = hn * gn_ref[:, hd * M_V_DIM:(hd + 1) * M_V_DIM]
        og = _sigmoid(o_ref[:, hd * M_V_DIM:(hd + 1) * M_V_DIM])
        out_ref[:, hd * M_V_DIM:(hd + 1) * M_V_DIM] = (hn * og).astype(out_ref.dtype)

        g_last = b_col[lc - 1:lc, :]
        a_col = g_last - b_col + ig_col
        m_new = jnp.maximum(g_last + m_prev, jnp.max(a_col, axis=0, keepdims=True))
        w_col = jnp.exp(a_col - m_new)
        decay = jnp.exp(g_last + m_prev - m_new)
        kw = k * w_col
        c_ref[hd] = decay * c_prev + _dot(kw.T.astype(BF16), v)
        n_ref[hd] = decay * n_prev + jnp.sum(kw, axis=0, keepdims=True)
        m_ref[hd] = jnp.broadcast_to(m_new, (1, LANES))


def _mlstm(zm, conv_w, conv_b, gate_b, out_norm, *, batch, seq, lc):
    t = zm.shape[0]
    nc = seq // lc
    qkw = 2 * M_QK_W
    return pl.pallas_call(
        functools.partial(_mlstm_kernel, lc=lc),
        out_shape=jax.ShapeDtypeStruct((t, M_V_W), BF16),
        grid=(batch, nc),
        in_specs=[
            pl.BlockSpec((lc, qkw), lambda b, c: (b * nc + c, 0)),
            pl.BlockSpec((lc, M_V_W), lambda b, c: (b * nc + c, qkw // M_V_W)),
            pl.BlockSpec((lc, M_V_W), lambda b, c: (b * nc + c, qkw // M_V_W + 1)),
            pl.BlockSpec((lc, GATE_W), lambda b, c: (b * nc + c, (qkw + 2 * M_V_W) // GATE_W)),
            pl.BlockSpec((M_CONV, qkw), lambda b, c: (0, 0)),
            pl.BlockSpec((1, qkw), lambda b, c: (0, 0)),
            pl.BlockSpec((1, GATE_W), lambda b, c: (0, 0)),
            pl.BlockSpec((1, M_V_W), lambda b, c: (0, 0)),
        ],
        out_specs=pl.BlockSpec((lc, M_V_W), lambda b, c: (b * nc + c, 0)),
        scratch_shapes=[
            pltpu.VMEM((lc + SUBLANES, qkw), F32),
            pltpu.VMEM((M_HEADS, M_QK_DIM, M_V_DIM), F32),
            pltpu.VMEM((M_HEADS, 1, M_QK_DIM), F32),
            pltpu.VMEM((M_HEADS, 1, LANES), F32),
        ],
        compiler_params=_cparams("arbitrary", "arbitrary"),
        name="mlstm",
    )(zm, zm, zm, zm, conv_w, conv_b, gate_b, out_norm)


MOBA_GROUP = 4
MOBA_HEADS = 2
MASK_BIG = 1e30


def _moba_kernel(q_ref, k_ref, v_ref, o_ref, kaug_ref, vt_ref, kmean_ref, qaug_ref, m_ref, l_ref,
                 acc_ref, s_ref, *, nblk):
    qi = pl.program_id(2)
    blk = MOBA_BLOCK
    hd = A_HEAD_DIM
    grp = MOBA_GROUP
    heads = range(MOBA_HEADS)
    c = (A_HEAD_DIM ** -0.5) * 1.4426950408889634

    @pl.when(qi == 0)
    def _():
        lane = lax.broadcasted_iota(jnp.int32, (blk, hd), 1)

        def prep(g, carry):
            for gg in range(grp):
                j = g * grp + gg
                r0 = pl.multiple_of(j * blk, blk)
                onehot = (lane == j).astype(BF16)
                for h in heads:
                    kj = k_ref[pl.ds(r0, blk), h * hd:(h + 1) * hd]
                    kaug_ref[h, pl.ds(r0, blk), 0:hd] = kj
                    kaug_ref[h, pl.ds(r0, blk), hd:2 * hd] = onehot
                    kmean_ref[h, pl.ds(j, 1), :] = jnp.mean(kj.astype(F32), axis=0, keepdims=True)
                    vt_ref[h, g, :, gg * blk:(gg + 1) * blk] = (
                        v_ref[pl.ds(r0, blk), h * hd:(h + 1) * hd].astype(F32).T.astype(BF16))
            return carry
        lax.fori_loop(0, nblk // grp, prep, 0)

    blk_id = lax.broadcasted_iota(jnp.int32, (nblk, blk), 0)
    blk_f = blk_id.astype(F32)
    past = blk_id < qi
    for h in heads:
        q = q_ref[:, h * hd:(h + 1) * hd]
        sc = lax.dot_general(kmean_ref[h], q.astype(F32), (((1,), (1,)), ((), ())),
                             preferred_element_type=F32, precision=lax.Precision.HIGHEST)
        sc = jnp.where(past, sc, NEG_INF)
        sel = jnp.zeros(sc.shape, jnp.bool_)
        for _ in range(MOBA_TOPK):
            mx = jnp.max(sc, axis=0, keepdims=True)
            first = jnp.min(jnp.where(sc == mx, blk_f, float(nblk)), axis=0, keepdims=True)
            pick = blk_f == first
            sel = sel | (pick & past)
            sc = jnp.where(pick, NEG_INF, sc)
        bias = jnp.where(sel | (blk_id == qi), 0.0, -MASK_BIG)
        bias = jnp.concatenate([bias, jnp.zeros((hd - nblk, blk), F32)], axis=0)
        qaug_ref[h, :, 0:hd] = q
        qaug_ref[h, :, hd:2 * hd] = bias.T.astype(BF16)

    m_ref[...] = jnp.full(m_ref.shape, -MASK_BIG, F32)
    l_ref[...] = jnp.zeros(l_ref.shape, F32)
    acc_ref[...] = jnp.zeros(acc_ref.shape, F32)

    def logits(g, h):
        c0 = pl.multiple_of(g * (grp * blk), grp * blk)
        return _dot_nt(kaug_ref[h, pl.ds(c0, grp * blk), :], qaug_ref[h])

    def consume(g, h, causal):
        sg = s_ref[h]
        if causal:
            kpos = g * (grp * blk) + lax.broadcasted_iota(jnp.int32, sg.shape, 0)
            qpos = qi * blk + lax.broadcasted_iota(jnp.int32, sg.shape, 1)
            sg = jnp.where(kpos <= qpos, sg, -MASK_BIG)
        m_old = m_ref[h]
        m_new = jnp.maximum(m_old, jnp.max(sg, axis=0, keepdims=True))
        alpha = jnp.exp2((m_old - m_new) * c)
        pg = jnp.exp2((sg - m_new) * c)
        l_ref[h] = alpha * l_ref[h] + jnp.sum(pg, axis=0, keepdims=True)
        acc_ref[h] = alpha * acc_ref[h] + _dot(vt_ref[h, g], pg.astype(BF16))
        m_ref[h] = m_new

    last = qi // grp
    for h in heads:
        s_ref[h] = logits(0, h)

    def body(g, carry):
        nxt = [logits(g + 1, h) for h in heads]
        for h in heads:
            consume(g, h, False)
        for h in heads:
            s_ref[h] = nxt[h]
        return carry
    lax.fori_loop(0, last, body, 0)
    for h in heads:
        consume(last, h, True)

    for h in heads:
        o_ref[:, h * hd:(h + 1) * hd] = (acc_ref[h] / l_ref[h]).T.astype(o_ref.dtype)


def _moba(qkv, *, batch, seq):
    t = qkv.shape[0]
    nblk = seq // MOBA_BLOCK
    hd = A_HEAD_DIM
    nh = MOBA_HEADS
    hw = nh * hd
    assert nblk % MOBA_GROUP == 0 and nblk <= hd and A_HEADS % nh == 0
    ngrp = A_HEADS // nh
    return pl.pallas_call(
        functools.partial(_moba_kernel, nblk=nblk),
        out_shape=jax.ShapeDtypeStruct((t, A_W), BF16),
        grid=(batch, ngrp, nblk),
        in_specs=[
            pl.BlockSpec((MOBA_BLOCK, hw), lambda b, h, i: (b * nblk + i, h)),
            pl.BlockSpec((seq, hw), lambda b, h, i: (b, ngrp + h)),
            pl.BlockSpec((seq, hw), lambda b, h, i: (b, 2 * ngrp + h)),
        ],
        out_specs=pl.BlockSpec((MOBA_BLOCK, hw), lambda b, h, i: (b * nblk + i, h)),
        scratch_shapes=[
            pltpu.VMEM((nh, seq, 2 * hd), BF16),
            pltpu.VMEM((nh, nblk // MOBA_GROUP, hd, MOBA_GROUP * MOBA_BLOCK), BF16),
            pltpu.VMEM((nh, nblk, hd), F32),
            pltpu.VMEM((nh, MOBA_BLOCK, 2 * hd), BF16),
            pltpu.VMEM((nh, 1, MOBA_BLOCK), F32),
            pltpu.VMEM((nh, 1, MOBA_BLOCK), F32),
            pltpu.VMEM((nh, hd, MOBA_BLOCK), F32),
            pltpu.VMEM((nh, MOBA_GROUP * MOBA_BLOCK, MOBA_BLOCK), F32),
        ],
        compiler_params=_cparams("arbitrary", "arbitrary", "arbitrary"),
        name="moba",
    )(qkv, qkv, qkv)


def _merge_kernel(h_ref, hm_ref, ha_ref, wgm_ref, wga_ref, wm_ref, wa_ref, o_ref):
    h = h_ref[...]
    gm = _sigmoid(_dot(h, wgm_ref[...]))
    ga = _sigmoid(_dot(h, wga_ref[...]))
    um = _dot(hm_ref[...], wm_ref[...])
    ua = _dot(ha_ref[...], wa_ref[...])
    o_ref[...] = (gm * um + ga * ua).astype(o_ref.dtype)


def _merge(h, hm, ha, wgm, wga, wm, wa, *, tm, tn):
    t, d = h.shape
    tm = min(tm, t)
    return pl.pallas_call(
        _merge_kernel,
        out_shape=jax.ShapeDtypeStruct((t, d), BF16),
        grid=(t // tm, d // tn),
        in_specs=[
            pl.BlockSpec((tm, d), lambda i, j: (i, 0)),
            pl.BlockSpec((tm, M_V_W), lambda i, j: (i, 0)),
            pl.BlockSpec((tm, A_W), lambda i, j: (i, 0)),
            pl.BlockSpec((d, tn), lambda i, j: (0, j)),
            pl.BlockSpec((d, tn), lambda i, j: (0, j)),
            pl.BlockSpec((M_V_W, tn), lambda i, j: (0, j)),
            pl.BlockSpec((A_W, tn), lambda i, j: (0, j)),
        ],
        out_specs=pl.BlockSpec((tm, tn), lambda i, j: (i, j)),
        compiler_params=_cparams("arbitrary", "arbitrary"),
        name="merge",
    )(h, hm, ha, wgm, wga, wm, wa)


def _outproj_kernel(x_ref, a_ref, w_ref, o_ref):
    o_ref[...] = x_ref[...] + _dot(a_ref[...], w_ref[...])


def _outproj(x, a, w, *, tm, tn):
    t, d = x.shape
    k = a.shape[1]
    tm = min(tm, t)
    return pl.pallas_call(
        _outproj_kernel,
        out_shape=jax.ShapeDtypeStruct((t, d), F32),
        grid=(t // tm, d // tn),
        in_specs=[
            pl.BlockSpec((tm, tn), lambda i, j: (i, j)),
            pl.BlockSpec((tm, k), lambda i, j: (i, 0)),
            pl.BlockSpec((k, tn), lambda i, j: (0, j)),
        ],
        out_specs=pl.BlockSpec((tm, tn), lambda i, j: (i, j)),
        compiler_params=_cparams("arbitrary", "arbitrary"),
        name="outproj",
    )(x, a, w)


def _ffn_kernel(x_ref, g_ref, w1_ref, w2_ref, o_ref, h_ref):
    @pl.when(pl.program_id(1) == 0)
    def _():
        h_ref[...] = _rmsnorm(x_ref[...], g_ref[...]).astype(BF16)
        o_ref[...] = x_ref[...]

    u = jnp.maximum(_dot(h_ref[...], w1_ref[...]), 0.0)
    o_ref[...] += _dot((u * u).astype(BF16), w2_ref[...])


def _ffn(x, g, w1, w2, *, tm, tf):
    t, d = x.shape
    dff = w1.shape[1]
    tm = min(tm, t)
    return pl.pallas_call(
        _ffn_kernel,
        out_shape=jax.ShapeDtypeStruct((t, d), F32),
        grid=(t // tm, dff // tf),
        in_specs=[
            pl.BlockSpec((tm, d), lambda i, f: (i, 0)),
            pl.BlockSpec((1, d), lambda i, f: (0, 0)),
            pl.BlockSpec((d, tf), lambda i, f: (0, f)),
            pl.BlockSpec((tf, d), lambda i, f: (f, 0)),
        ],
        out_specs=pl.BlockSpec((tm, d), lambda i, f: (i, 0)),
        scratch_shapes=[pltpu.VMEM((tm, d), BF16)],
        compiler_params=_cparams("arbitrary", "arbitrary"),
        name="ffn",
    )(x, g, w1, w2)


def _ple_kernel(x_ref, p_ref, g_ref, wg_ref, wp_ref, gf_ref, o_ref, *, final):
    x = x_ref[...]
    h = _rmsnorm(x, g_ref[...]).astype(BF16)
    gate = _sigmoid(_dot(h, wg_ref[...]))
    emb = _dot(p_ref[...].astype(BF16), wp_ref[...])
    y = x + gate * emb
    o_ref[...] = _rmsnorm(y, gf_ref[...]) if final else y


def _ple(x, p, g, wg, wp, gf, *, tm, final):
    t, d = x.shape
    tm = min(tm, t)
    return pl.pallas_call(
        functools.partial(_ple_kernel, final=final),
        out_shape=jax.ShapeDtypeStruct((t, d), F32),
        grid=(t // tm,),
        in_specs=[
            pl.BlockSpec((tm, d), lambda i: (i, 0)),
            pl.BlockSpec((tm, PLE_DIM), lambda i: (i, 0)),
            pl.BlockSpec((1, d), lambda i: (0, 0)),
            pl.BlockSpec((d, d), lambda i: (0, 0)),
            pl.BlockSpec((PLE_DIM, d), lambda i: (0, 0)),
            pl.BlockSpec((1, d), lambda i: (0, 0)),
        ],
        out_specs=pl.BlockSpec((tm, d), lambda i: (i, 0)),
        compiler_params=_cparams("arbitrary"),
        name="ple",
    )(x, p, g, wg, wp, gf)


def _layer(x2d, p2d, pos2d, batch, seq, final_norm, final, attn_norm, w_in, b_if, conv_w, conv_b,
           m_out_norm, w_up_m, w_up_a, w_out, mlp_norm, w_ff1, w_ff2, ple_norm, w_ple_gate,
           w_ple_proj):
    d = D_MODEL
    o_if = 2 * M_QK_W + 2 * M_V_W
    o_a = o_if + 2 * M_HEADS
    o_g = o_a + 3 * A_W
    w_gate = jnp.pad(w_in[:, o_if:o_a], ((0, 0), (0, GATE_W - 2 * M_HEADS)))
    w_m = jnp.concatenate([w_in[:, :o_if], w_gate], axis=1).astype(BF16)
    w_a = w_in[:, o_a:o_g].astype(BF16)
    w_gm = w_in[:, o_g:o_g + d].astype(BF16)
    w_ga = w_in[:, o_g + d:].astype(BF16)
    gate_b = jnp.pad(b_if.reshape(1, 2 * M_HEADS), ((0, 0), (0, GATE_W - 2 * M_HEADS)))
    g_attn = attn_norm.reshape(1, d)

    half = ROPE_DIM // 2
    inv_freq = ROPE_THETA ** (-jnp.arange(half, dtype=F32) * 2.0 / ROPE_DIM)
    freq = jnp.concatenate([-inv_freq, inv_freq, jnp.zeros((A_HEAD_DIM - ROPE_DIM,), F32)]).reshape(1, -1)

    zm, h = _inproj_m(x2d, g_attn, w_m, tm=1024, tn=ZM_W // 5)
    qkv = _inproj_a(h, w_a, pos2d, freq, tm=1024)
    hm = _mlstm(zm, conv_w, conv_b.reshape(1, -1), gate_b, m_out_norm.reshape(1, -1),
                batch=batch, seq=seq, lc=256)
    ha = _moba(qkv, batch=batch, seq=seq)
    merged = _merge(h, hm, ha, w_gm, w_ga, w_up_m.astype(BF16), w_up_a.astype(BF16),
                    tm=1024, tn=512)
    x1 = _outproj(x2d, merged, w_out.astype(BF16), tm=1024, tn=1024)
    x2 = _ffn(x1, mlp_norm.reshape(1, d), w_ff1.astype(BF16), w_ff2.astype(BF16), tm=1024, tf=512)
    return _ple(x2, p2d, ple_norm.reshape(1, d), w_ple_gate.astype(BF16), w_ple_proj.astype(BF16),
                final_norm.reshape(1, d), tm=512, final=final)


def kernel(x, p, positions, attn_norm, w_in, b_if, conv_w, conv_b, m_out_norm, w_up_m, w_up_a,
           w_out, mlp_norm, w_ff1, w_ff2, ple_norm, w_ple_gate, w_ple_proj, final_norm):
    batch, seq, d = x.shape
    depth = w_in.shape[0]
    t = batch * seq
    x2d = x.reshape(t, d)
    pos2d = positions.reshape(t, 1)
    for i in range(depth):
        x2d = _layer(
            x2d, p[i].reshape(t, PLE_DIM), pos2d, batch, seq, final_norm, i == depth - 1,
            attn_norm[i], w_in[i], b_if[i], conv_w[i], conv_b[i], m_out_norm[i], w_up_m[i],
            w_up_a[i], w_out[i], mlp_norm[i], w_ff1[i], w_ff2[i], ple_norm[i], w_ple_gate[i],
            w_ple_proj[i])
    return x2d.reshape(batch, seq, d)
```

```python
import functools

import jax
import jax.numpy as jnp
from jax import lax
from jax.experimental import pallas as pl
from jax.experimental.pallas import tpu as pltpu

D_MODEL = 2048
PLE_DIM = 256
M_HEADS = 4
M_QK_DIM = 128
M_V_DIM = 256
M_CONV = 4
A_HEADS = 8
A_HEAD_DIM = 128
MOBA_BLOCK = 256
MOBA_TOPK = 3
ROPE_THETA = 500000.0
ROPE_DIM = A_HEAD_DIM // 4
D_FF = 4 * D_MODEL
EPS = 1e-6

M_QK_W = M_HEADS * M_QK_DIM
M_V_W = M_HEADS * M_V_DIM
A_W = A_HEADS * A_HEAD_DIM

LANES = 128
SUBLANES = 8
GATE_W = LANES
ZM_W = 2 * M_QK_W + 2 * M_V_W + GATE_W
VMEM_LIMIT = 56 * 1024 * 1024

F32 = jnp.float32
BF16 = jnp.bfloat16
NEG_INF = float("-inf")


def _cparams(*sem):
    return pltpu.CompilerParams(dimension_semantics=sem, vmem_limit_bytes=VMEM_LIMIT)


def _rmsnorm(x, g):
    ms = jnp.mean(x * x, axis=-1, keepdims=True)
    return (x * lax.rsqrt(ms + EPS)) * g


def _sigmoid(x):
    return 1.0 / (1.0 + jnp.exp(-x))


def _dot(a, b):
    return jnp.dot(a, b, preferred_element_type=F32)


def _dot_nt(a, b):
    return lax.dot_general(a, b, (((1,), (1,)), ((), ())), preferred_element_type=F32)


def _resident(shape):
    return pl.BlockSpec(shape, lambda *_: (0,) * len(shape), pipeline_mode=pl.Buffered(1))


def _inproj_m_kernel(x_ref, g_ref, w_ref, o_ref, h_ref):
    h = _rmsnorm(x_ref[...], g_ref[...]).astype(BF16)
    h_ref[...] = h
    o_ref[...] = _dot(h, w_ref[...])


def _inproj_m(x, g, w, *, tm):
    t, d = x.shape
    n = w.shape[1]
    tm = min(tm, t)
    return pl.pallas_call(
        _inproj_m_kernel,
        out_shape=(jax.ShapeDtypeStruct((t, n), F32), jax.ShapeDtypeStruct((t, d), BF16)),
        grid=(t // tm,),
        in_specs=[
            pl.BlockSpec((tm, d), lambda i: (i, 0)),
            _resident((1, d)),
            _resident((d, n)),
        ],
        out_specs=(pl.BlockSpec((tm, n), lambda i: (i, 0)),
                   pl.BlockSpec((tm, d), lambda i: (i, 0))),
        compiler_params=_cparams("arbitrary"),
        name="inproj_m",
    )(x, g, w)


def _inproj_a_kernel(h_ref, w_ref, pos_ref, freq_ref, o_ref):
    h = h_ref[...]
    ang = pos_ref[...].astype(F32) * freq_ref[...]
    c = jnp.cos(ang)
    s = jnp.sin(ang)
    lane = lax.broadcasted_iota(jnp.int32, c.shape, 1)
    half = ROPE_DIM // 2
    for part in range(2):
        z = _dot(h, w_ref[:, part * A_W:(part + 1) * A_W])
        for hd in range(A_HEADS):
            zh = z[:, hd * A_HEAD_DIM:(hd + 1) * A_HEAD_DIM]
            partner = jnp.where(lane < half,
                                pltpu.roll(zh, A_HEAD_DIM - half, 1),
                                pltpu.roll(zh, half, 1))
            rot = jnp.where(lane < ROPE_DIM, zh * c + partner * s, zh)
            c0 = part * A_W + hd * A_HEAD_DIM
            o_ref[:, c0:c0 + A_HEAD_DIM] = rot.astype(o_ref.dtype)
    o_ref[:, 2 * A_W:] = _dot(h, w_ref[:, 2 * A_W:]).astype(o_ref.dtype)


def _inproj_a(h, w, pos, freq, *, tm):
    t, d = h.shape
    n = w.shape[1]
    tm = min(tm, t)
    return pl.pallas_call(
        _inproj_a_kernel,
        out_shape=jax.ShapeDtypeStruct((t, n), BF16),
        grid=(t // tm,),
        in_specs=[
            pl.BlockSpec((tm, d), lambda i: (i, 0)),
            _resident((d, n)),
            pl.BlockSpec((tm, 1), lambda i: (i, 0)),
            _resident((1, A_HEAD_DIM)),
        ],
        out_specs=pl.BlockSpec((tm, n), lambda i: (i, 0)),
        compiler_params=_cparams("arbitrary"),
        name="inproj_a",
    )(h, w, pos, freq)


def _mlstm_kernel(qk_ref, v_ref, o_ref, gate_ref, cw_ref, cb_ref, gb_ref, gn_ref, out_ref,
                  buf_ref, c_ref, n_ref, m_ref, *, lc):
    ci = pl.program_id(1)
    pad = SUBLANES

    @pl.when(ci == 0)
    def _():
        buf_ref[0:pad, :] = jnp.zeros((pad, 2 * M_QK_W), F32)
        c_ref[...] = jnp.zeros(c_ref.shape, F32)
        n_ref[...] = jnp.zeros(n_ref.shape, F32)
        m_ref[...] = jnp.zeros(m_ref.shape, F32)

    buf_ref[pad:pad + lc, :] = qk_ref[...]
    y = cb_ref[...] + cw_ref[M_CONV - 1:M_CONV, :] * buf_ref[pad:pad + lc, :]
    for j in range(1, M_CONV):
        y = y + cw_ref[M_CONV - 1 - j:M_CONV - j, :] * buf_ref[pad - j:pad - j + lc, :]
    buf_ref[0:pad, :] = buf_ref[lc:lc + pad, :]
    qk = y * _sigmoid(y)

    gates = gate_ref[...] + gb_ref[...]
    lf = jnp.minimum(gates, 0.0) - jnp.log1p(jnp.exp(-jnp.abs(gates)))
    row = lax.broadcasted_iota(jnp.int32, (lc, lc), 0)
    col = lax.broadcasted_iota(jnp.int32, (lc, lc), 1)
    causal = row >= col
    tri = causal.astype(F32)
    bcum = jnp.dot(tri, lf, preferred_element_type=F32, precision=lax.Precision.HIGHEST)
    gates_t = gates.T
    bcum_t = bcum.T

    for hd in range(M_HEADS):
        q = qk[:, hd * M_QK_DIM:(hd + 1) * M_QK_DIM]
        k = qk[:, M_QK_W + hd * M_QK_DIM:M_QK_W + (hd + 1) * M_QK_DIM] * (M_QK_DIM ** -0.5)
        v = v_ref[:, hd * M_V_DIM:(hd + 1) * M_V_DIM].astype(BF16)
        q_bf = q.astype(BF16)
        k_bf = k.astype(BF16)
        ig_col = gates[:, hd:hd + 1]
        ig_row = gates_t[hd:hd + 1, :]
        b_col = bcum[:, M_HEADS + hd:M_HEADS + hd + 1]
        b_row = bcum_t[M_HEADS + hd:M_HEADS + hd + 1, :]
        m_prev = m_ref[hd][:, 0:1]
        c_prev = c_ref[hd]
        n_prev = n_ref[hd]

        dmat = jnp.where(causal, b_col - b_row + ig_row, NEG_INF)
        inter = b_col + m_prev
        m_t = jnp.maximum(inter, jnp.max(dmat, axis=-1, keepdims=True))
        s = _dot_nt(q_bf, k_bf) * jnp.exp(dmat - m_t)
        e_inter = jnp.exp(inter - m_t)
        q_c = _dot(q_bf, c_prev.astype(BF16))
        q_n = jnp.sum(q * n_prev, axis=-1, keepdims=True)
        num = e_inter * q_c + _dot(s.astype(BF16), v)
        den = e_inter * q_n + jnp.sum(s, axis=-1, keepdims=True)
        h = num / jnp.maximum(jnp.abs(den), jnp.exp(-m_t))

        hn = h * lax.rsqrt(jnp.mean(h * h, axis=-1, keepdims=True) + EPS)
        hn = hn * gn_ref[:, hd * M_V_DIM:(hd + 1) * M_V_DIM]
        og = _sigmoid(o_ref[:, hd * M_V_DIM:(hd + 1) * M_V_DIM])
        out_ref[:, hd * M_V_DIM:(hd + 1) * M_V_DIM] = (hn * og).astype(out_ref.dtype)

        g_last = b_col[lc - 1:lc, :]
        a_col = g_last - b_col + ig_col
        m_new = jnp.maximum(g_last + m_prev, jnp.max(a_col, axis=0, keepdims=True))
        w_col = jnp.exp(a_col - m_new)
        decay = jnp.exp(g_last + m_prev - m_new)
        kw = k * w_col
        c_ref[hd] = decay * c_prev + _dot(kw.T.astype(BF16), v)
        n_ref[hd] = decay * n_prev + jnp.sum(kw, axis=0, keepdims=True)
        m_ref[hd] = jnp.broadcast_to(m_new, (1, LANES))


def _mlstm(zm, conv_w, conv_b, gate_b, out_norm, *, batch, seq, lc):
    t = zm.shape[0]
    nc = seq // lc
    qkw = 2 * M_QK_W
    return pl.pallas_call(
        functools.partial(_mlstm_kernel, lc=lc),
        out_shape=jax.ShapeDtypeStruct((t, M_V_W), BF16),
        grid=(batch, nc),
        in_specs=[
            pl.BlockSpec((lc, qkw), lambda b, c: (b * nc + c, 0)),
            pl.BlockSpec((lc, M_V_W), lambda b, c: (b * nc + c, qkw // M_V_W)),
            pl.BlockSpec((lc, M_V_W), lambda b, c: (b * nc + c, qkw // M_V_W + 1)),
            pl.BlockSpec((lc, GATE_W), lambda b, c: (b * nc + c, (qkw + 2 * M_V_W) // GATE_W)),
            pl.BlockSpec((M_CONV, qkw), lambda b, c: (0, 0)),
            pl.BlockSpec((1, qkw), lambda b, c: (0, 0)),
            pl.BlockSpec((1, GATE_W), lambda b, c: (0, 0)),
            pl.BlockSpec((1, M_V_W), lambda b, c: (0, 0)),
        ],
        out_specs=pl.BlockSpec((lc, M_V_W), lambda b, c: (b * nc + c, 0)),
        scratch_shapes=[
            pltpu.VMEM((lc + SUBLANES, qkw), F32),
            pltpu.VMEM((M_HEADS, M_QK_DIM, M_V_DIM), F32),
            pltpu.VMEM((M_HEADS, 1, M_QK_DIM), F32),
            pltpu.VMEM((M_HEADS, 1, LANES), F32),
        ],
        compiler_params=_cparams("arbitrary", "arbitrary"),
        name="mlstm",
    )(zm, zm, zm, zm, conv_w, conv_b, gate_b, out_norm)


MOBA_GROUP = 4
MOBA_HEADS = 4
MASK_BIG = 1e30


def _moba_kernel(q_ref, k_ref, v_ref, o_ref, kaug_ref, vt_ref, kmean_ref, ksplit_ref, qaug_ref,
                 m_ref, l_ref, acc_ref, s_ref, *, nblk):
    qi = pl.program_id(2)
    blk = MOBA_BLOCK
    hd = A_HEAD_DIM
    grp = MOBA_GROUP
    heads = range(MOBA_HEADS)
    c = (A_HEAD_DIM ** -0.5) * 1.4426950408889634

    @pl.when(qi == 0)
    def _():
        lane = lax.broadcasted_iota(jnp.int32, (blk, hd), 1)

        def prep(g, carry):
            for gg in range(grp):
                j = g * grp + gg
                r0 = pl.multiple_of(j * blk, blk)
                onehot = (lane == j).astype(BF16)
                for h in heads:
                    kj = k_ref[pl.ds(r0, blk), h * hd:(h + 1) * hd]
                    kaug_ref[h, pl.ds(r0, blk), 0:hd] = kj
                    kaug_ref[h, pl.ds(r0, blk), hd:2 * hd] = onehot
                    kmean_ref[h, pl.ds(j, 1), :] = jnp.mean(kj.astype(F32), axis=0, keepdims=True)
                    vt_ref[h, g, :, gg * blk:(gg + 1) * blk] = (
                        v_ref[pl.ds(r0, blk), h * hd:(h + 1) * hd].astype(F32).T.astype(BF16))
            return carry
        lax.fori_loop(0, nblk // grp, prep, 0)
        for h in heads:
            rest = kmean_ref[h]
            for part in range(3):
                term = rest.astype(BF16)
                ksplit_ref[h, part] = term
                rest = rest - term.astype(F32)

    blk_id = lax.broadcasted_iota(jnp.int32, (nblk, blk), 0)
    blk_f = blk_id.astype(F32)
    past = blk_id < qi
    for h in heads:
        q = q_ref[:, h * hd:(h + 1) * hd]
        sc = (_dot_nt(ksplit_ref[h, 0], q) + _dot_nt(ksplit_ref[h, 1], q)) + _dot_nt(ksplit_ref[h, 2], q)
        sc = jnp.where(past, sc, NEG_INF)
        sel = jnp.zeros(sc.shape, jnp.bool_)
        for _ in range(MOBA_TOPK):
            mx = jnp.max(sc, axis=0, keepdims=True)
            first = jnp.min(jnp.where(sc == mx, blk_f, float(nblk)), axis=0, keepdims=True)
            pick = blk_f == first
            sel = sel | (pick & past)
            sc = jnp.where(pick, NEG_INF, sc)
        bias = jnp.where(sel | (blk_id == qi), 0.0, -MASK_BIG)
        bias = jnp.concatenate([bias, jnp.zeros((hd - nblk, blk), F32)], axis=0)
        qaug_ref[h, :, 0:hd] = q
        qaug_ref[h, :, hd:2 * hd] = bias.T.astype(BF16)

    m_ref[...] = jnp.full(m_ref.shape, -MASK_BIG, F32)
    l_ref[...] = jnp.zeros(l_ref.shape, F32)
    acc_ref[...] = jnp.zeros(acc_ref.shape, F32)

    def logits(g, h):
        c0 = pl.multiple_of(g * (grp * blk), grp * blk)
        return _dot_nt(kaug_ref[h, pl.ds(c0, grp * blk), :], qaug_ref[h])

    def consume(g, h, causal):
        sg = s_ref[h]
        if causal:
            kpos = g * (grp * blk) + lax.broadcasted_iota(jnp.int32, sg.shape, 0)
            qpos = qi * blk + lax.broadcasted_iota(jnp.int32, sg.shape, 1)
            sg = jnp.where(kpos <= qpos, sg, -MASK_BIG)
        m_old = m_ref[h]
        m_new = jnp.maximum(m_old, jnp.max(sg, axis=0, keepdims=True))
        alpha = jnp.exp2((m_old - m_new) * c)
        pg = jnp.exp2((sg - m_new) * c)
        l_ref[h] = alpha * l_ref[h] + jnp.sum(pg, axis=0, keepdims=True)
        acc_ref[h] = alpha * acc_ref[h] + _dot(vt_ref[h, g], pg.astype(BF16))
        m_ref[h] = m_new

    last = qi // grp
    for h in heads:
        s_ref[h] = logits(0, h)

    def body(g, carry):
        nxt = [logits(g + 1, h) for h in heads]
        for h in heads:
            consume(g, h, False)
        for h in heads:
            s_ref[h] = nxt[h]
        return carry
    lax.fori_loop(0, last, body, 0)
    for h in heads:
        consume(last, h, True)

    for h in heads:
        o_ref[:, h * hd:(h + 1) * hd] = (acc_ref[h] / l_ref[h]).T.astype(o_ref.dtype)


def _moba(qkv, *, batch, seq):
    t = qkv.shape[0]
    nblk = seq // MOBA_BLOCK
    hd = A_HEAD_DIM
    nh = MOBA_HEADS
    hw = nh * hd
    assert nblk % MOBA_GROUP == 0 and nblk <= hd and A_HEADS % nh == 0
    ngrp = A_HEADS // nh
    return pl.pallas_call(
        functools.partial(_moba_kernel, nblk=nblk),
        out_shape=jax.ShapeDtypeStruct((t, A_W), BF16),
        grid=(batch, ngrp, nblk),
        in_specs=[
            pl.BlockSpec((MOBA_BLOCK, hw), lambda b, h, i: (b * nblk + i, h)),
            pl.BlockSpec((seq, hw), lambda b, h, i: (b, ngrp + h), pipeline_mode=pl.Buffered(1)),
            pl.BlockSpec((seq, hw), lambda b, h, i: (b, 2 * ngrp + h), pipeline_mode=pl.Buffered(1)),
        ],
        out_specs=pl.BlockSpec((MOBA_BLOCK, hw), lambda b, h, i: (b * nblk + i, h)),
        scratch_shapes=[
            pltpu.VMEM((nh, seq, 2 * hd), BF16),
            pltpu.VMEM((nh, nblk // MOBA_GROUP, hd, MOBA_GROUP * MOBA_BLOCK), BF16),
            pltpu.VMEM((nh, nblk, hd), F32),
            pltpu.VMEM((nh, 3, nblk, hd), BF16),
            pltpu.VMEM((nh, MOBA_BLOCK, 2 * hd), BF16),
            pltpu.VMEM((nh, 1, MOBA_BLOCK), F32),
            pltpu.VMEM((nh, 1, MOBA_BLOCK), F32),
            pltpu.VMEM((nh, hd, MOBA_BLOCK), F32),
            pltpu.VMEM((nh, MOBA_GROUP * MOBA_BLOCK, MOBA_BLOCK), F32),
        ],
        compiler_params=_cparams("arbitrary", "arbitrary", "arbitrary"),
        name="moba",
    )(qkv, qkv, qkv)


def _merge_kernel(h_ref, hm_ref, ha_ref, wgm_ref, wga_ref, wm_ref, wa_ref, o_ref):
    h = h_ref[...]
    gm = _sigmoid(_dot(h, wgm_ref[...]))
    ga = _sigmoid(_dot(h, wga_ref[...]))
    um = _dot(hm_ref[...], wm_ref[...])
    ua = _dot(ha_ref[...], wa_ref[...])
    o_ref[...] = (gm * um + ga * ua).astype(o_ref.dtype)


def _merge(h, hm, ha, wgm, wga, wm, wa, *, tm, tn):
    t, d = h.shape
    tm = min(tm, t)
    return pl.pallas_call(
        _merge_kernel,
        out_shape=jax.ShapeDtypeStruct((t, d), BF16),
        grid=(t // tm, d // tn),
        in_specs=[
            pl.BlockSpec((tm, d), lambda i, j: (i, 0)),
            pl.BlockSpec((tm, M_V_W), lambda i, j: (i, 0)),
            pl.BlockSpec((tm, A_W), lambda i, j: (i, 0)),
            pl.BlockSpec((d, tn), lambda i, j: (0, j)),
            pl.BlockSpec((d, tn), lambda i, j: (0, j)),
            pl.BlockSpec((M_V_W, tn), lambda i, j: (0, j)),
            pl.BlockSpec((A_W, tn), lambda i, j: (0, j)),
        ],
        out_specs=pl.BlockSpec((tm, tn), lambda i, j: (i, j)),
        compiler_params=_cparams("arbitrary", "arbitrary"),
        name="merge",
    )(h, hm, ha, wgm, wga, wm, wa)


def _outproj_kernel(x_ref, a_ref, w_ref, o_ref):
    o_ref[...] = x_ref[...] + _dot(a_ref[...], w_ref[...])


def _outproj(x, a, w, *, tm, tn):
    t, d = x.shape
    k = a.shape[1]
    tm = min(tm, t)
    return pl.pallas_call(
        _outproj_kernel,
        out_shape=jax.ShapeDtypeStruct((t, d), F32),
        grid=(t // tm, d // tn),
        in_specs=[
            pl.BlockSpec((tm, tn), lambda i, j: (i, j)),
            pl.BlockSpec((tm, k), lambda i, j: (i, 0)),
            pl.BlockSpec((k, tn), lambda i, j: (0, j)),
        ],
        out_specs=pl.BlockSpec((tm, tn), lambda i, j: (i, j)),
        compiler_params=_cparams("arbitrary", "arbitrary"),
        name="outproj",
    )(x, a, w)


def _ffn_kernel(x_ref, g_ref, w1_ref, w2_ref, o_ref, h_ref):
    @pl.when(pl.program_id(1) == 0)
    def _():
        h_ref[...] = _rmsnorm(x_ref[...], g_ref[...]).astype(BF16)
        o_ref[...] = x_ref[...]

    u = jnp.maximum(_dot(h_ref[...], w1_ref[...]), 0.0)
    o_ref[...] += _dot((u * u).astype(BF16), w2_ref[...])


def _ffn(x, g, w1, w2, *, tm, tf):
    t, d = x.shape
    dff = w1.shape[1]
    tm = min(tm, t)
    return pl.pallas_call(
        _ffn_kernel,
        out_shape=jax.ShapeDtypeStruct((t, d), F32),
        grid=(t // tm, dff // tf),
        in_specs=[
            pl.BlockSpec((tm, d), lambda i, f: (i, 0)),
            pl.BlockSpec((1, d), lambda i, f: (0, 0)),
            pl.BlockSpec((d, tf), lambda i, f: (0, f)),
            pl.BlockSpec((tf, d), lambda i, f: (f, 0)),
        ],
        out_specs=pl.BlockSpec((tm, d), lambda i, f: (i, 0)),
        scratch_shapes=[pltpu.VMEM((tm, d), BF16)],
        compiler_params=_cparams("arbitrary", "arbitrary"),
        name="ffn",
    )(x, g, w1, w2)


def _ple_kernel(x_ref, p_ref, g_ref, wg_ref, wp_ref, gf_ref, o_ref, *, final):
    x = x_ref[...]
    h = _rmsnorm(x, g_ref[...]).astype(BF16)
    gate = _sigmoid(_dot(h, wg_ref[...]))
    emb = _dot(p_ref[...].astype(BF16), wp_ref[...])
    y = x + gate * emb
    o_ref[...] = _rmsnorm(y, gf_ref[...]) if final else y


def _ple(x, p, g, wg, wp, gf, *, tm, final):
    t, d = x.shape
    tm = min(tm, t)
    return pl.pallas_call(
        functools.partial(_ple_kernel, final=final),
        out_shape=jax.ShapeDtypeStruct((t, d), F32),
        grid=(t // tm,),
        in_specs=[
            pl.BlockSpec((tm, d), lambda i: (i, 0)),
            pl.BlockSpec((tm, PLE_DIM), lambda i: (i, 0)),
            pl.BlockSpec((1, d), lambda i: (0, 0)),
            pl.BlockSpec((d, d), lambda i: (0, 0)),
            pl.BlockSpec((PLE_DIM, d), lambda i: (0, 0)),
            pl.BlockSpec((1, d), lambda i: (0, 0)),
        ],
        out_specs=pl.BlockSpec((tm, d), lambda i: (i, 0)),
        compiler_params=_cparams("arbitrary"),
        name="ple",
    )(x, p, g, wg, wp, gf)


def _layer(x2d, p2d, pos2d, batch, seq, final_norm, final, attn_norm, w_in, b_if, conv_w, conv_b,
           m_out_norm, w_up_m, w_up_a, w_out, mlp_norm, w_ff1, w_ff2, ple_norm, w_ple_gate,
           w_ple_proj):
    d = D_MODEL
    o_if = 2 * M_QK_W + 2 * M_V_W
    o_a = o_if + 2 * M_HEADS
    o_g = o_a + 3 * A_W
    w_gate = jnp.pad(w_in[:, o_if:o_a], ((0, 0), (0, GATE_W - 2 * M_HEADS)))
    w_m = jnp.concatenate([w_in[:, :o_if], w_gate], axis=1).astype(BF16)
    w_a = w_in[:, o_a:o_g].astype(BF16)
    w_gm = w_in[:, o_g:o_g + d].astype(BF16)
    w_ga = w_in[:, o_g + d:].astype(BF16)
    gate_b = jnp.pad(b_if.reshape(1, 2 * M_HEADS), ((0, 0), (0, GATE_W - 2 * M_HEADS)))
    g_attn = attn_norm.reshape(1, d)

    half = ROPE_DIM // 2
    inv_freq = ROPE_THETA ** (-jnp.arange(half, dtype=F32) * 2.0 / ROPE_DIM)
    freq = jnp.concatenate([-inv_freq, inv_freq, jnp.zeros((A_HEAD_DIM - ROPE_DIM,), F32)]).reshape(1, -1)

    zm, h = _inproj_m(x2d, g_attn, w_m, tm=512)
    qkv = _inproj_a(h, w_a, pos2d, freq, tm=512)
    hm = _mlstm(zm, conv_w, conv_b.reshape(1, -1), gate_b, m_out_norm.reshape(1, -1),
                batch=batch, seq=seq, lc=256)
    ha = _moba(qkv, batch=batch, seq=seq)
    merged = _merge(h, hm, ha, w_gm, w_ga, w_up_m.astype(BF16), w_up_a.astype(BF16),
                    tm=1024, tn=512)
    x1 = _outproj(x2d, merged, w_out.astype(BF16), tm=1024, tn=1024)
    x2 = _ffn(x1, mlp_norm.reshape(1, d), w_ff1.astype(BF16), w_ff2.astype(BF16), tm=1024, tf=512)
    return _ple(x2, p2d, ple_norm.reshape(1, d), w_ple_gate.astype(BF16), w_ple_proj.astype(BF16),
                final_norm.reshape(1, d), tm=512, final=final)


def kernel(x, p, positions, attn_norm, w_in, b_if, conv_w, conv_b, m_out_norm, w_up_m, w_up_a,
           w_out, mlp_norm, w_ff1, w_ff2, ple_norm, w_ple_gate, w_ple_proj, final_norm):
    batch, seq, d = x.shape
    depth = w_in.shape[0]
    t = batch * seq
    x2d = x.reshape(t, d)
    pos2d = positions.reshape(t, 1)
    for i in range(depth):
        x2d = _layer(
            x2d, p[i].reshape(t, PLE_DIM), pos2d, batch, seq, final_norm, i == depth - 1,
            attn_norm[i], w_in[i], b_if[i], conv_w[i], conv_b[i], m_out_norm[i], w_up_m[i],
            w_up_a[i], w_out[i], mlp_norm[i], w_ff1[i], w_ff2[i], ple_norm[i], w_ple_gate[i],
            w_ple_proj[i])
    return x2d.reshape(batch, seq, d)
```

```python
import functools

import jax
import jax.numpy as jnp
from jax import lax
from jax.experimental import pallas as pl
from jax.experimental.pallas import tpu as pltpu

D_MODEL = 2048
PLE_DIM = 256
M_HEADS = 4
M_QK_DIM = 128
M_V_DIM = 256
M_CONV = 4
A_HEADS = 8
A_HEAD_DIM = 128
MOBA_BLOCK = 256
MOBA_TOPK = 3
ROPE_THETA = 500000.0
ROPE_DIM = A_HEAD_DIM // 4
D_FF = 4 * D_MODEL
EPS = 1e-6

M_QK_W = M_HEADS * M_QK_DIM
M_V_W = M_HEADS * M_V_DIM
A_W = A_HEADS * A_HEAD_DIM

LANES = 128
SUBLANES = 8
GATE_W = LANES
ZM_W = 2 * M_QK_W + 2 * M_V_W + GATE_W
VMEM_LIMIT = 56 * 1024 * 1024

F32 = jnp.float32
BF16 = jnp.bfloat16
NEG_INF = float("-inf")


def _cparams(*sem):
    return pltpu.CompilerParams(dimension_semantics=sem, vmem_limit_bytes=VMEM_LIMIT)


def _rmsnorm(x, g):
    ms = jnp.mean(x * x, axis=-1, keepdims=True)
    return (x * lax.rsqrt(ms + EPS)) * g


def _sigmoid(x):
    return 1.0 / (1.0 + jnp.exp(-x))


def _dot(a, b):
    return jnp.dot(a, b, preferred_element_type=F32)


def _dot_nt(a, b):
    return lax.dot_general(a, b, (((1,), (1,)), ((), ())), preferred_element_type=F32)


def _resident(shape):
    return pl.BlockSpec(shape, lambda *_: (0,) * len(shape), pipeline_mode=pl.Buffered(1))


def _inproj_m_kernel(x_ref, g_ref, w_ref, o_ref, h_ref):
    h = _rmsnorm(x_ref[...], g_ref[...]).astype(BF16)
    h_ref[...] = h
    o_ref[...] = _dot(h, w_ref[...])


def _inproj_m(x, g, w, *, tm):
    t, d = x.shape
    n = w.shape[1]
    tm = min(tm, t)
    return pl.pallas_call(
        _inproj_m_kernel,
        out_shape=(jax.ShapeDtypeStruct((t, n), F32), jax.ShapeDtypeStruct((t, d), BF16)),
        grid=(t // tm,),
        in_specs=[
            pl.BlockSpec((tm, d), lambda i: (i, 0)),
            _resident((1, d)),
            _resident((d, n)),
        ],
        out_specs=(pl.BlockSpec((tm, n), lambda i: (i, 0)),
                   pl.BlockSpec((tm, d), lambda i: (i, 0))),
        compiler_params=_cparams("arbitrary"),
        name="inproj_m",
    )(x, g, w)


def _inproj_a_kernel(h_ref, w_ref, pos_ref, freq_ref, o_ref):
    h = h_ref[...]
    ang = pos_ref[...].astype(F32) * freq_ref[...]
    c = jnp.cos(ang)
    s = jnp.sin(ang)
    lane = lax.broadcasted_iota(jnp.int32, c.shape, 1)
    half = ROPE_DIM // 2
    for part in range(2):
        z = _dot(h, w_ref[:, part * A_W:(part + 1) * A_W])
        for hd in range(A_HEADS):
            zh = z[:, hd * A_HEAD_DIM:(hd + 1) * A_HEAD_DIM]
            partner = jnp.where(lane < half,
                                pltpu.roll(zh, A_HEAD_DIM - half, 1),
                                pltpu.roll(zh, half, 1))
            rot = jnp.where(lane < ROPE_DIM, zh * c + partner * s, zh)
            c0 = part * A_W + hd * A_HEAD_DIM
            o_ref[:, c0:c0 + A_HEAD_DIM] = rot.astype(o_ref.dtype)
    o_ref[:, 2 * A_W:] = _dot(h, w_ref[:, 2 * A_W:]).astype(o_ref.dtype)


def _inproj_a(h, w, pos, freq, *, tm):
    t, d = h.shape
    n = w.shape[1]
    tm = min(tm, t)
    return pl.pallas_call(
        _inproj_a_kernel,
        out_shape=jax.ShapeDtypeStruct((t, n), BF16),
        grid=(t // tm,),
        in_specs=[
            pl.BlockSpec((tm, d), lambda i: (i, 0)),
            _resident((d, n)),
            pl.BlockSpec((tm, 1), lambda i: (i, 0)),
            _resident((1, A_HEAD_DIM)),
        ],
        out_specs=pl.BlockSpec((tm, n), lambda i: (i, 0)),
        compiler_params=_cparams("arbitrary"),
        name="inproj_a",
    )(h, w, pos, freq)


def _mlstm_kernel(qk_ref, v_ref, o_ref, gate_ref, cw_ref, cb_ref, gb_ref, gn_ref, out_ref,
                  buf_ref, c_ref, n_ref, m_ref, *, lc):
    ci = pl.program_id(1)
    pad = SUBLANES

    @pl.when(ci == 0)
    def _():
        buf_ref[0:pad, :] = jnp.zeros((pad, 2 * M_QK_W), F32)
        c_ref[...] = jnp.zeros(c_ref.shape, F32)
        n_ref[...] = jnp.zeros(n_ref.shape, F32)
        m_ref[...] = jnp.zeros(m_ref.shape, F32)

    buf_ref[pad:pad + lc, :] = qk_ref[...]
    y = cb_ref[...] + cw_ref[M_CONV - 1:M_CONV, :] * buf_ref[pad:pad + lc, :]
    for j in range(1, M_CONV):
        y = y + cw_ref[M_CONV - 1 - j:M_CONV - j, :] * buf_ref[pad - j:pad - j + lc, :]
    buf_ref[0:pad, :] = buf_ref[lc:lc + pad, :]
    qk = y * _sigmoid(y)

    gates = gate_ref[...] + gb_ref[...]
    lf = jnp.minimum(gates, 0.0) - jnp.log1p(jnp.exp(-jnp.abs(gates)))
    row = lax.broadcasted_iota(jnp.int32, (lc, lc), 0)
    col = lax.broadcasted_iota(jnp.int32, (lc, lc), 1)
    causal = row >= col
    tri = causal.astype(F32)
    bcum = jnp.dot(tri, lf, preferred_element_type=F32, precision=lax.Precision.HIGHEST)
    gates_t = gates.T
    bcum_t = bcum.T

    heads = range(M_HEADS)
    q_bf, k_f32, k_bf, v_bf, qk_dot, qc_dot, qn_sum = [], [], [], [], [], [], []
    for hd in heads:
        q = qk[:, hd * M_QK_DIM:(hd + 1) * M_QK_DIM]
        k = qk[:, M_QK_W + hd * M_QK_DIM:M_QK_W + (hd + 1) * M_QK_DIM] * (M_QK_DIM ** -0.5)
        q_bf.append(q.astype(BF16))
        k_f32.append(k)
        k_bf.append(k.astype(BF16))
        v_bf.append(v_ref[:, hd * M_V_DIM:(hd + 1) * M_V_DIM].astype(BF16))
        qk_dot.append(_dot_nt(q_bf[hd], k_bf[hd]))
        qc_dot.append(_dot(q_bf[hd], c_ref[hd].astype(BF16)))
        qn_sum.append(jnp.sum(q * n_ref[hd], axis=-1, keepdims=True))

    for hd in heads:
        ig_col = gates[:, hd:hd + 1]
        ig_row = gates_t[hd:hd + 1, :]
        b_col = bcum[:, M_HEADS + hd:M_HEADS + hd + 1]
        b_row = bcum_t[M_HEADS + hd:M_HEADS + hd + 1, :]
        m_prev = m_ref[hd][:, 0:1]

        dmat = jnp.where(causal, b_col - b_row + ig_row, NEG_INF)
        inter = b_col + m_prev
        m_t = jnp.maximum(inter, jnp.max(dmat, axis=-1, keepdims=True))
        s = qk_dot[hd] * jnp.exp(dmat - m_t)
        e_inter = jnp.exp(inter - m_t)
        num = e_inter * qc_dot[hd] + _dot(s.astype(BF16), v_bf[hd])
        den = e_inter * qn_sum[hd] + jnp.sum(s, axis=-1, keepdims=True)
        h = num / jnp.maximum(jnp.abs(den), jnp.exp(-m_t))

        hn = h * lax.rsqrt(jnp.mean(h * h, axis=-1, keepdims=True) + EPS)
        hn = hn * gn_ref[:, hd * M_V_DIM:(hd + 1) * M_V_DIM]
        og = _sigmoid(o_ref[:, hd * M_V_DIM:(hd + 1) * M_V_DIM])
        out_ref[:, hd * M_V_DIM:(hd + 1) * M_V_DIM] = (hn * og).astype(out_ref.dtype)

    for hd in heads:
        ig_col = gates[:, hd:hd + 1]
        b_col = bcum[:, M_HEADS + hd:M_HEADS + hd + 1]
        m_prev = m_ref[hd][:, 0:1]
        g_last = b_col[lc - 1:lc, :]
        a_col = g_last - b_col + ig_col
        m_new = jnp.maximum(g_last + m_prev, jnp.max(a_col, axis=0, keepdims=True))
        w_col = jnp.exp(a_col - m_new)
        decay = jnp.exp(g_last + m_prev - m_new)
        kw = k_f32[hd] * w_col
        c_ref[hd] = decay * c_ref[hd] + _dot(kw.T.astype(BF16), v_bf[hd])
        n_ref[hd] = decay * n_ref[hd] + jnp.sum(kw, axis=0, keepdims=True)
        m_ref[hd] = jnp.broadcast_to(m_new, (1, LANES))


def _mlstm(zm, conv_w, conv_b, gate_b, out_norm, *, batch, seq, lc):
    t = zm.shape[0]
    nc = seq // lc
    qkw = 2 * M_QK_W
    return pl.pallas_call(
        functools.partial(_mlstm_kernel, lc=lc),
        out_shape=jax.ShapeDtypeStruct((t, M_V_W), BF16),
        grid=(batch, nc),
        in_specs=[
            pl.BlockSpec((lc, qkw), lambda b, c: (b * nc + c, 0)),
            pl.BlockSpec((lc, M_V_W), lambda b, c: (b * nc + c, qkw // M_V_W)),
            pl.BlockSpec((lc, M_V_W), lambda b, c: (b * nc + c, qkw // M_V_W + 1)),
            pl.BlockSpec((lc, GATE_W), lambda b, c: (b * nc + c, (qkw + 2 * M_V_W) // GATE_W)),
            pl.BlockSpec((M_CONV, qkw), lambda b, c: (0, 0)),
            pl.BlockSpec((1, qkw), lambda b, c: (0, 0)),
            pl.BlockSpec((1, GATE_W), lambda b, c: (0, 0)),
            pl.BlockSpec((1, M_V_W), lambda b, c: (0, 0)),
        ],
        out_specs=pl.BlockSpec((lc, M_V_W), lambda b, c: (b * nc + c, 0)),
        scratch_shapes=[
            pltpu.VMEM((lc + SUBLANES, qkw), F32),
            pltpu.VMEM((M_HEADS, M_QK_DIM, M_V_DIM), F32),
            pltpu.VMEM((M_HEADS, 1, M_QK_DIM), F32),
            pltpu.VMEM((M_HEADS, 1, LANES), F32),
        ],
        compiler_params=_cparams("arbitrary", "arbitrary"),
        name="mlstm",
    )(zm, zm, zm, zm, conv_w, conv_b, gate_b, out_norm)


MOBA_GROUP = 4
MOBA_HEADS = 4
MASK_BIG = 1e30
VT_ROWS = A_HEAD_DIM + 16


def _moba_kernel(q_ref, k_ref, v_ref, o_ref, kaug_ref, vt_ref, kmean_ref, ksplit_ref, qaug_ref,
                 m_ref, alpha_ref, acc_ref, s_ref, p_ref, *, nblk):
    qi = pl.program_id(2)
    blk = MOBA_BLOCK
    hd = A_HEAD_DIM
    grp = MOBA_GROUP
    nh = MOBA_HEADS
    heads = range(nh)
    c = (A_HEAD_DIM ** -0.5) * 1.4426950408889634

    @pl.when(qi == 0)
    def _():
        lane = lax.broadcasted_iota(jnp.int32, (blk, hd), 1)
        row = lax.broadcasted_iota(jnp.int32, (VT_ROWS - hd, grp * blk), 0)
        ones_rows = (row == 0).astype(BF16)

        def prep(g, carry):
            for h in heads:
                vt_ref[h, g, hd:VT_ROWS, :] = ones_rows
            for gg in range(grp):
                j = g * grp + gg
                r0 = pl.multiple_of(j * blk, blk)
                for h in heads:
                    kj = k_ref[pl.ds(r0, blk), h * hd:(h + 1) * hd]
                    kaug_ref[h, pl.ds(r0, blk), 0:hd] = kj
                    kaug_ref[h, pl.ds(r0, blk), hd:2 * hd] = (lane == nblk * h + j).astype(BF16)
                    kmean_ref[h, pl.ds(j, 1), :] = jnp.mean(kj.astype(F32), axis=0, keepdims=True)
                    vt_ref[h, g, 0:hd, gg * blk:(gg + 1) * blk] = (
                        v_ref[pl.ds(r0, blk), h * hd:(h + 1) * hd].astype(F32).T.astype(BF16))
            return carry
        lax.fori_loop(0, nblk // grp, prep, 0)
        for h in heads:
            rest = kmean_ref[h]
            for part in range(3):
                term = rest.astype(BF16)
                ksplit_ref[h, part] = term
                rest = rest - term.astype(F32)

    blk_id = lax.broadcasted_iota(jnp.int32, (nblk, blk), 0)
    blk_f = blk_id.astype(F32)
    past = blk_id < qi
    biases = []
    for h in heads:
        q = q_ref[:, h * hd:(h + 1) * hd]
        sc = (_dot_nt(ksplit_ref[h, 0], q) + _dot_nt(ksplit_ref[h, 1], q)) + _dot_nt(ksplit_ref[h, 2], q)
        sc = jnp.where(past, sc, NEG_INF)
        sel = jnp.zeros(sc.shape, jnp.bool_)
        for _ in range(MOBA_TOPK):
            mx = jnp.max(sc, axis=0, keepdims=True)
            first = jnp.min(jnp.where(sc == mx, blk_f, float(nblk)), axis=0, keepdims=True)
            pick = blk_f == first
            sel = sel | (pick & past)
            sc = jnp.where(pick, NEG_INF, sc)
        biases.append(jnp.where(sel | (blk_id == qi), 0.0, -MASK_BIG))
        qaug_ref[h, :, 0:hd] = q
    if nh * nblk < hd:
        biases.append(jnp.zeros((hd - nh * nblk, blk), F32))
    bias_t = jnp.concatenate(biases, axis=0).T.astype(BF16)
    for h in heads:
        qaug_ref[h, :, hd:2 * hd] = bias_t

    def logits(g, h):
        c0 = pl.multiple_of(g * (grp * blk), grp * blk)
        return _dot_nt(kaug_ref[h, pl.ds(c0, grp * blk), :], qaug_ref[h])

    def softmax(g, h, causal):
        sg = s_ref[h]
        if causal:
            kpos = g * (grp * blk) + lax.broadcasted_iota(jnp.int32, sg.shape, 0)
            qpos = qi * blk + lax.broadcasted_iota(jnp.int32, sg.shape, 1)
            sg = jnp.where(kpos <= qpos, sg, -MASK_BIG)
        m_old = m_ref[h]
        m_new = jnp.maximum(m_old, jnp.max(sg, axis=0, keepdims=True))
        alpha_ref[h] = jnp.exp2((m_old - m_new) * c)
        p_ref[h] = jnp.exp2(((sg - m_new) * c).astype(BF16))
        m_ref[h] = m_new

    def accumulate(g, h):
        acc_ref[h] = alpha_ref[h] * acc_ref[h] + _dot(vt_ref[h, g], p_ref[h])

    m_ref[...] = jnp.full(m_ref.shape, -MASK_BIG, F32)
    acc_ref[...] = jnp.zeros(acc_ref.shape, F32)

    last = qi // grp
    for h in heads:
        s_ref[h] = logits(0, h)

    def body(g, carry):
        nxt = [logits(g + 1, h) for h in heads]
        for h in heads:
            softmax(g, h, False)
            accumulate(g, h)
        for h in heads:
            s_ref[h] = nxt[h]
        return carry
    lax.fori_loop(0, last, body, 0)
    for h in heads:
        softmax(last, h, True)
        accumulate(last, h)

    for h in heads:
        acc = acc_ref[h]
        o_ref[:, h * hd:(h + 1) * hd] = (acc[0:hd] / acc[hd:hd + 1]).T.astype(o_ref.dtype)


def _moba(qkv, *, batch, seq):
    t = qkv.shape[0]
    nblk = seq // MOBA_BLOCK
    hd = A_HEAD_DIM
    nh = MOBA_HEADS
    hw = nh * hd
    assert nblk % MOBA_GROUP == 0 and nh * nblk <= hd and A_HEADS % nh == 0
    ngrp = A_HEADS // nh
    return pl.pallas_call(
        functools.partial(_moba_kernel, nblk=nblk),
        out_shape=jax.ShapeDtypeStruct((t, A_W), BF16),
        grid=(batch, ngrp, nblk),
        in_specs=[
            pl.BlockSpec((MOBA_BLOCK, hw), lambda b, h, i: (b * nblk + i, h)),
            pl.BlockSpec((seq, hw), lambda b, h, i: (b, ngrp + h), pipeline_mode=pl.Buffered(1)),
            pl.BlockSpec((seq, hw), lambda b, h, i: (b, 2 * ngrp + h), pipeline_mode=pl.Buffered(1)),
        ],
        out_specs=pl.BlockSpec((MOBA_BLOCK, hw), lambda b, h, i: (b * nblk + i, h)),
        scratch_shapes=[
            pltpu.VMEM((nh, seq, 2 * hd), BF16),
            pltpu.VMEM((nh, nblk // MOBA_GROUP, VT_ROWS, MOBA_GROUP * MOBA_BLOCK), BF16),
            pltpu.VMEM((nh, nblk, hd), F32),
            pltpu.VMEM((nh, 3, nblk, hd), BF16),
            pltpu.VMEM((nh, MOBA_BLOCK, 2 * hd), BF16),
            pltpu.VMEM((nh, 1, MOBA_BLOCK), F32),
            pltpu.VMEM((nh, 1, MOBA_BLOCK), F32),
            pltpu.VMEM((nh, VT_ROWS, MOBA_BLOCK), F32),
            pltpu.VMEM((nh, MOBA_GROUP * MOBA_BLOCK, MOBA_BLOCK), F32),
            pltpu.VMEM((nh, MOBA_GROUP * MOBA_BLOCK, MOBA_BLOCK), BF16),
        ],
        compiler_params=_cparams("arbitrary", "arbitrary", "arbitrary"),
        name="moba",
    )(qkv, qkv, qkv)


def _merge_kernel(h_ref, hm_ref, ha_ref, wgm_ref, wga_ref, wm_ref, wa_ref, o_ref):
    h = h_ref[...]
    gm = _sigmoid(_dot(h, wgm_ref[...]))
    ga = _sigmoid(_dot(h, wga_ref[...]))
    um = _dot(hm_ref[...], wm_ref[...])
    ua = _dot(ha_ref[...], wa_ref[...])
    o_ref[...] = (gm * um + ga * ua).astype(o_ref.dtype)


def _merge(h, hm, ha, wgm, wga, wm, wa, *, tm, tn):
    t, d = h.shape
    tm = min(tm, t)
    return pl.pallas_call(
        _merge_kernel,
        out_shape=jax.ShapeDtypeStruct((t, d), BF16),
        grid=(t // tm, d // tn),
        in_specs=[
            pl.BlockSpec((tm, d), lambda i, j: (i, 0)),
            pl.BlockSpec((tm, M_V_W), lambda i, j: (i, 0)),
            pl.BlockSpec((tm, A_W), lambda i, j: (i, 0)),
            pl.BlockSpec((d, tn), lambda i, j: (0, j)),
            pl.BlockSpec((d, tn), lambda i, j: (0, j)),
            pl.BlockSpec((M_V_W, tn), lambda i, j: (0, j)),
            pl.BlockSpec((A_W, tn), lambda i, j: (0, j)),
        ],
        out_specs=pl.BlockSpec((tm, tn), lambda i, j: (i, j)),
        compiler_params=_cparams("arbitrary", "arbitrary"),
        name="merge",
    )(h, hm, ha, wgm, wga, wm, wa)


def _outproj_kernel(x_ref, a_ref, w_ref, g_ref, o_ref, h_ref):
    y = x_ref[...] + _dot(a_ref[...], w_ref[...])
    o_ref[...] = y
    h_ref[...] = _rmsnorm(y, g_ref[...]).astype(BF16)


def _outproj(x, a, w, g, *, tm):
    t, d = x.shape
    k = a.shape[1]
    tm = min(tm, t)
    return pl.pallas_call(
        _outproj_kernel,
        out_shape=(jax.ShapeDtypeStruct((t, d), F32), jax.ShapeDtypeStruct((t, d), BF16)),
        grid=(t // tm,),
        in_specs=[
            pl.BlockSpec((tm, d), lambda i: (i, 0)),
            pl.BlockSpec((tm, k), lambda i: (i, 0)),
            _resident((k, d)),
            _resident((1, d)),
        ],
        out_specs=(pl.BlockSpec((tm, d), lambda i: (i, 0)),
                   pl.BlockSpec((tm, d), lambda i: (i, 0))),
        compiler_params=_cparams("arbitrary"),
        name="outproj",
    )(x, a, w, g)


def _ffn_kernel(x_ref, h_ref, w1_ref, w2_ref, o_ref):
    @pl.when(pl.program_id(1) == 0)
    def _():
        o_ref[...] = x_ref[...]

    u = jnp.maximum(_dot(h_ref[...], w1_ref[...]), 0.0)
    o_ref[...] += _dot((u * u).astype(BF16), w2_ref[...])


def _ffn(x, h, w1, w2, *, tm, tf):
    t, d = x.shape
    dff = w1.shape[1]
    tm = min(tm, t)
    return pl.pallas_call(
        _ffn_kernel,
        out_shape=jax.ShapeDtypeStruct((t, d), F32),
        grid=(t // tm, dff // tf),
        in_specs=[
            pl.BlockSpec((tm, d), lambda i, f: (i, 0)),
            pl.BlockSpec((tm, d), lambda i, f: (i, 0)),
            pl.BlockSpec((d, tf), lambda i, f: (0, f)),
            pl.BlockSpec((tf, d), lambda i, f: (f, 0)),
        ],
        out_specs=pl.BlockSpec((tm, d), lambda i, f: (i, 0)),
        compiler_params=_cparams("arbitrary", "arbitrary"),
        name="ffn",
    )(x, h, w1, w2)


def _ple_kernel(x_ref, p_ref, g_ref, wg_ref, wp_ref, gf_ref, o_ref, *, final):
    x = x_ref[...]
    h = _rmsnorm(x, g_ref[...]).astype(BF16)
    gate = _sigmoid(_dot(h, wg_ref[...]))
    emb = _dot(p_ref[...].astype(BF16), wp_ref[...])
    y = x + gate * emb
    o_ref[...] = _rmsnorm(y, gf_ref[...]) if final else y


def _ple(x, p, g, wg, wp, gf, *, tm, final):
    t, d = x.shape
    tm = min(tm, t)
    return pl.pallas_call(
        functools.partial(_ple_kernel, final=final),
        out_shape=jax.ShapeDtypeStruct((t, d), F32),
        grid=(t // tm,),
        in_specs=[
            pl.BlockSpec((tm, d), lambda i: (i, 0)),
            pl.BlockSpec((tm, PLE_DIM), lambda i: (i, 0)),
            pl.BlockSpec((1, d), lambda i: (0, 0)),
            pl.BlockSpec((d, d), lambda i: (0, 0)),
            pl.BlockSpec((PLE_DIM, d), lambda i: (0, 0)),
            pl.BlockSpec((1, d), lambda i: (0, 0)),
        ],
        out_specs=pl.BlockSpec((tm, d), lambda i: (i, 0)),
        compiler_params=_cparams("arbitrary"),
        name="ple",
    )(x, p, g, wg, wp, gf)


def _layer(x2d, p2d, pos2d, batch, seq, final_norm, final, attn_norm, w_in, b_if, conv_w, conv_b,
           m_out_norm, w_up_m, w_up_a, w_out, mlp_norm, w_ff1, w_ff2, ple_norm, w_ple_gate,
           w_ple_proj):
    d = D_MODEL
    o_if = 2 * M_QK_W + 2 * M_V_W
    o_a = o_if + 2 * M_HEADS
    o_g = o_a + 3 * A_W
    w_gate = jnp.pad(w_in[:, o_if:o_a], ((0, 0), (0, GATE_W - 2 * M_HEADS)))
    w_m = jnp.concatenate([w_in[:, :o_if], w_gate], axis=1).astype(BF16)
    w_a = w_in[:, o_a:o_g].astype(BF16)
    w_gm = w_in[:, o_g:o_g + d].astype(BF16)
    w_ga = w_in[:, o_g + d:].astype(BF16)
    gate_b = jnp.pad(b_if.reshape(1, 2 * M_HEADS), ((0, 0), (0, GATE_W - 2 * M_HEADS)))
    g_attn = attn_norm.reshape(1, d)

    half = ROPE_DIM // 2
    inv_freq = ROPE_THETA ** (-jnp.arange(half, dtype=F32) * 2.0 / ROPE_DIM)
    freq = jnp.concatenate([-inv_freq, inv_freq, jnp.zeros((A_HEAD_DIM - ROPE_DIM,), F32)]).reshape(1, -1)

    zm, h = _inproj_m(x2d, g_attn, w_m, tm=512)
    qkv = _inproj_a(h, w_a, pos2d, freq, tm=512)
    hm = _mlstm(zm, conv_w, conv_b.reshape(1, -1), gate_b, m_out_norm.reshape(1, -1),
                batch=batch, seq=seq, lc=256)
    ha = _moba(qkv, batch=batch, seq=seq)
    merged = _merge(h, hm, ha, w_gm, w_ga, w_up_m.astype(BF16), w_up_a.astype(BF16),
                    tm=1024, tn=512)
    x1, h1 = _outproj(x2d, merged, w_out.astype(BF16), mlp_norm.reshape(1, d), tm=512)
    x2 = _ffn(x1, h1, w_ff1.astype(BF16), w_ff2.astype(BF16), tm=1024, tf=512)
    return _ple(x2, p2d, ple_norm.reshape(1, d), w_ple_gate.astype(BF16), w_ple_proj.astype(BF16),
                final_norm.reshape(1, d), tm=512, final=final)


def kernel(x, p, positions, attn_norm, w_in, b_if, conv_w, conv_b, m_out_norm, w_up_m, w_up_a,
           w_out, mlp_norm, w_ff1, w_ff2, ple_norm, w_ple_gate, w_ple_proj, final_norm):
    batch, seq, d = x.shape
    depth = w_in.shape[0]
    t = batch * seq
    x2d = x.reshape(t, d)
    pos2d = positions.reshape(t, 1)
    for i in range(depth):
        x2d = _layer(
            x2d, p[i].reshape(t, PLE_DIM), pos2d, batch, seq, final_norm, i == depth - 1,
            attn_norm[i], w_in[i], b_if[i], conv_w[i], conv_b[i], m_out_norm[i], w_up_m[i],
            w_up_a[i], w_out[i], mlp_norm[i], w_ff1[i], w_ff2[i], ple_norm[i], w_ple_gate[i],
            w_ple_proj[i])
    return x2d.reshape(batch, seq, d)
```

```python
import functools

import jax
import jax.numpy as jnp
from jax import lax
from jax.experimental import pallas as pl
from jax.experimental.pallas import tpu as pltpu

D_MODEL = 2048
PLE_DIM = 256
M_HEADS = 4
M_QK_DIM = 128
M_V_DIM = 256
M_CONV = 4
A_HEADS = 8
A_HEAD_DIM = 128
MOBA_BLOCK = 256
MOBA_TOPK = 3
ROPE_THETA = 500000.0
ROPE_DIM = A_HEAD_DIM // 4
D_FF = 4 * D_MODEL
EPS = 1e-6

M_QK_W = M_HEADS * M_QK_DIM
M_V_W = M_HEADS * M_V_DIM
A_W = A_HEADS * A_HEAD_DIM

LANES = 128
SUBLANES = 8
GATE_W = LANES
ZM_W = 2 * M_QK_W + 2 * M_V_W + GATE_W
VMEM_LIMIT = 56 * 1024 * 1024

F32 = jnp.float32
BF16 = jnp.bfloat16
NEG_INF = float("-inf")


def _cparams(*sem):
    return pltpu.CompilerParams(dimension_semantics=sem, vmem_limit_bytes=VMEM_LIMIT)


def _rmsnorm(x, g):
    ms = jnp.mean(x * x, axis=-1, keepdims=True)
    return (x * lax.rsqrt(ms + EPS)) * g


def _sigmoid(x):
    return 1.0 / (1.0 + jnp.exp(-x))


def _dot(a, b):
    return jnp.dot(a, b, preferred_element_type=F32)


def _dot_nt(a, b):
    return lax.dot_general(a, b, (((1,), (1,)), ((), ())), preferred_element_type=F32)


def _resident(shape):
    return pl.BlockSpec(shape, lambda *_: (0,) * len(shape), pipeline_mode=pl.Buffered(1))


def _inproj_m_kernel(x_ref, g_ref, w_ref, o_ref, h_ref):
    h = _rmsnorm(x_ref[...], g_ref[...]).astype(BF16)
    h_ref[...] = h
    o_ref[...] = _dot(h, w_ref[...])


def _inproj_m(x, g, w, *, tm):
    t, d = x.shape
    n = w.shape[1]
    tm = min(tm, t)
    return pl.pallas_call(
        _inproj_m_kernel,
        out_shape=(jax.ShapeDtypeStruct((t, n), F32), jax.ShapeDtypeStruct((t, d), BF16)),
        grid=(t // tm,),
        in_specs=[
            pl.BlockSpec((tm, d), lambda i: (i, 0)),
            _resident((1, d)),
            _resident((d, n)),
        ],
        out_specs=(pl.BlockSpec((tm, n), lambda i: (i, 0)),
                   pl.BlockSpec((tm, d), lambda i: (i, 0))),
        compiler_params=_cparams("arbitrary"),
        name="inproj_m",
    )(x, g, w)


def _inproj_a_kernel(h_ref, w_ref, pos_ref, freq_ref, o_ref):
    h = h_ref[...]
    ang = pos_ref[...].astype(F32) * freq_ref[...]
    c = jnp.cos(ang)
    s = jnp.sin(ang)
    lane = lax.broadcasted_iota(jnp.int32, c.shape, 1)
    half = ROPE_DIM // 2
    for part in range(2):
        z = _dot(h, w_ref[:, part * A_W:(part + 1) * A_W])
        for hd in range(A_HEADS):
            zh = z[:, hd * A_HEAD_DIM:(hd + 1) * A_HEAD_DIM]
            partner = jnp.where(lane < half,
                                pltpu.roll(zh, A_HEAD_DIM - half, 1),
                                pltpu.roll(zh, half, 1))
            rot = jnp.where(lane < ROPE_DIM, zh * c + partner * s, zh)
            c0 = part * A_W + hd * A_HEAD_DIM
            o_ref[:, c0:c0 + A_HEAD_DIM] = rot.astype(o_ref.dtype)
    o_ref[:, 2 * A_W:] = _dot(h, w_ref[:, 2 * A_W:]).astype(o_ref.dtype)


def _inproj_a(h, w, pos, freq, *, tm):
    t, d = h.shape
    n = w.shape[1]
    tm = min(tm, t)
    return pl.pallas_call(
        _inproj_a_kernel,
        out_shape=jax.ShapeDtypeStruct((t, n), BF16),
        grid=(t // tm,),
        in_specs=[
            pl.BlockSpec((tm, d), lambda i: (i, 0)),
            _resident((d, n)),
            pl.BlockSpec((tm, 1), lambda i: (i, 0)),
            _resident((1, A_HEAD_DIM)),
        ],
        out_specs=pl.BlockSpec((tm, n), lambda i: (i, 0)),
        compiler_params=_cparams("arbitrary"),
        name="inproj_a",
    )(h, w, pos, freq)


def _mlstm_kernel(qk_ref, v_ref, o_ref, gate_ref, cw_ref, cb_ref, gb_ref, gn_ref, out_ref,
                  buf_ref, c_ref, n_ref, m_ref, *, lc):
    ci = pl.program_id(1)
    pad = SUBLANES

    @pl.when(ci == 0)
    def _():
        buf_ref[0:pad, :] = jnp.zeros((pad, 2 * M_QK_W), F32)
        c_ref[...] = jnp.zeros(c_ref.shape, F32)
        n_ref[...] = jnp.zeros(n_ref.shape, F32)
        m_ref[...] = jnp.zeros(m_ref.shape, F32)

    buf_ref[pad:pad + lc, :] = qk_ref[...]
    y = cb_ref[...] + cw_ref[M_CONV - 1:M_CONV, :] * buf_ref[pad:pad + lc, :]
    for j in range(1, M_CONV):
        y = y + cw_ref[M_CONV - 1 - j:M_CONV - j, :] * buf_ref[pad - j:pad - j + lc, :]
    buf_ref[0:pad, :] = buf_ref[lc:lc + pad, :]
    qk = y * _sigmoid(y)

    gates = gate_ref[...] + gb_ref[...]
    lf = jnp.minimum(gates, 0.0) - jnp.log1p(jnp.exp(-jnp.abs(gates)))
    row = lax.broadcasted_iota(jnp.int32, (lc, lc), 0)
    col = lax.broadcasted_iota(jnp.int32, (lc, lc), 1)
    causal = row >= col
    tri = causal.astype(F32)
    bcum = jnp.dot(tri, lf, preferred_element_type=F32, precision=lax.Precision.HIGHEST)
    gates_t = gates.T
    bcum_t = bcum.T

    heads = range(M_HEADS)
    q_bf, k_f32, k_bf, v_bf, qk_dot, qc_dot, qn_sum = [], [], [], [], [], [], []
    for hd in heads:
        q = qk[:, hd * M_QK_DIM:(hd + 1) * M_QK_DIM]
        k = qk[:, M_QK_W + hd * M_QK_DIM:M_QK_W + (hd + 1) * M_QK_DIM] * (M_QK_DIM ** -0.5)
        q_bf.append(q.astype(BF16))
        k_f32.append(k)
        k_bf.append(k.astype(BF16))
        v_bf.append(v_ref[:, hd * M_V_DIM:(hd + 1) * M_V_DIM].astype(BF16))
        qk_dot.append(_dot_nt(q_bf[hd], k_bf[hd]))
        qc_dot.append(_dot(q_bf[hd], c_ref[hd].astype(BF16)))
        qn_sum.append(jnp.sum(q * n_ref[hd], axis=-1, keepdims=True))

    for hd in heads:
        ig_col = gates[:, hd:hd + 1]
        ig_row = gates_t[hd:hd + 1, :]
        b_col = bcum[:, M_HEADS + hd:M_HEADS + hd + 1]
        b_row = bcum_t[M_HEADS + hd:M_HEADS + hd + 1, :]
        m_prev = m_ref[hd][:, 0:1]

        dmat = jnp.where(causal, b_col - b_row + ig_row, NEG_INF)
        inter = b_col + m_prev
        m_t = jnp.maximum(inter, jnp.max(dmat, axis=-1, keepdims=True))
        s = qk_dot[hd] * jnp.exp(dmat - m_t)
        e_inter = jnp.exp(inter - m_t)
        num = e_inter * qc_dot[hd] + _dot(s.astype(BF16), v_bf[hd])
        den = e_inter * qn_sum[hd] + jnp.sum(s, axis=-1, keepdims=True)
        h = num / jnp.maximum(jnp.abs(den), jnp.exp(-m_t))

        hn = h * lax.rsqrt(jnp.mean(h * h, axis=-1, keepdims=True) + EPS)
        hn = hn * gn_ref[:, hd * M_V_DIM:(hd + 1) * M_V_DIM]
        og = _sigmoid(o_ref[:, hd * M_V_DIM:(hd + 1) * M_V_DIM])
        out_ref[:, hd * M_V_DIM:(hd + 1) * M_V_DIM] = (hn * og).astype(out_ref.dtype)

    for hd in heads:
        ig_col = gates[:, hd:hd + 1]
        b_col = bcum[:, M_HEADS + hd:M_HEADS + hd + 1]
        m_prev = m_ref[hd][:, 0:1]
        g_last = b_col[lc - 1:lc, :]
        a_col = g_last - b_col + ig_col
        m_new = jnp.maximum(g_last + m_prev, jnp.max(a_col, axis=0, keepdims=True))
        w_col = jnp.exp(a_col - m_new)
        decay = jnp.exp(g_last + m_prev - m_new)
        kw = k_f32[hd] * w_col
        c_ref[hd] = decay * c_ref[hd] + _dot(kw.T.astype(BF16), v_bf[hd])
        n_ref[hd] = decay * n_ref[hd] + jnp.sum(kw, axis=0, keepdims=True)
        m_ref[hd] = jnp.broadcast_to(m_new, (1, LANES))


def _mlstm(zm, conv_w, conv_b, gate_b, out_norm, *, batch, seq, lc):
    t = zm.shape[0]
    nc = seq // lc
    qkw = 2 * M_QK_W
    return pl.pallas_call(
        functools.partial(_mlstm_kernel, lc=lc),
        out_shape=jax.ShapeDtypeStruct((t, M_V_W), BF16),
        grid=(batch, nc),
        in_specs=[
            pl.BlockSpec((lc, qkw), lambda b, c: (b * nc + c, 0)),
            pl.BlockSpec((lc, M_V_W), lambda b, c: (b * nc + c, qkw // M_V_W)),
            pl.BlockSpec((lc, M_V_W), lambda b, c: (b * nc + c, qkw // M_V_W + 1)),
            pl.BlockSpec((lc, GATE_W), lambda b, c: (b * nc + c, (qkw + 2 * M_V_W) // GATE_W)),
            pl.BlockSpec((M_CONV, qkw), lambda b, c: (0, 0)),
            pl.BlockSpec((1, qkw), lambda b, c: (0, 0)),
            pl.BlockSpec((1, GATE_W), lambda b, c: (0, 0)),
            pl.BlockSpec((1, M_V_W), lambda b, c: (0, 0)),
        ],
        out_specs=pl.BlockSpec((lc, M_V_W), lambda b, c: (b * nc + c, 0)),
        scratch_shapes=[
            pltpu.VMEM((lc + SUBLANES, qkw), F32),
            pltpu.VMEM((M_HEADS, M_QK_DIM, M_V_DIM), F32),
            pltpu.VMEM((M_HEADS, 1, M_QK_DIM), F32),
            pltpu.VMEM((M_HEADS, 1, LANES), F32),
        ],
        compiler_params=_cparams("arbitrary", "arbitrary"),
        name="mlstm",
    )(zm, zm, zm, zm, conv_w, conv_b, gate_b, out_norm)


MOBA_GROUP = 4
MOBA_HEADS = 4
MASK_BIG = 1e30
VT_ROWS = A_HEAD_DIM + 16


def _moba_kernel(q_ref, k_ref, v_ref, o_ref, kaug_ref, vt_ref, kmean_ref, ksplit_ref, qaug_ref,
                 m_ref, alpha_ref, acc_ref, s_ref, p_ref, *, nblk):
    qi = pl.program_id(2)
    blk = MOBA_BLOCK
    hd = A_HEAD_DIM
    grp = MOBA_GROUP
    nh = MOBA_HEADS
    heads = range(nh)
    c = (A_HEAD_DIM ** -0.5) * 1.4426950408889634

    @pl.when(qi == 0)
    def _():
        lane = lax.broadcasted_iota(jnp.int32, (blk, hd), 1)
        row = lax.broadcasted_iota(jnp.int32, (VT_ROWS - hd, grp * blk), 0)
        ones_rows = (row == 0).astype(BF16)

        def prep(g, carry):
            for h in heads:
                vt_ref[h, g, hd:VT_ROWS, :] = ones_rows
            for gg in range(grp):
                j = g * grp + gg
                r0 = pl.multiple_of(j * blk, blk)
                for h in heads:
                    kj = k_ref[pl.ds(r0, blk), h * hd:(h + 1) * hd]
                    kaug_ref[h, pl.ds(r0, blk), 0:hd] = kj
                    kaug_ref[h, pl.ds(r0, blk), hd:2 * hd] = (lane == nblk * h + j).astype(BF16)
                    kmean_ref[h, pl.ds(j, 1), :] = jnp.mean(kj.astype(F32), axis=0, keepdims=True)
                    vt_ref[h, g, 0:hd, gg * blk:(gg + 1) * blk] = (
                        v_ref[pl.ds(r0, blk), h * hd:(h + 1) * hd].astype(F32).T.astype(BF16))
            return carry
        lax.fori_loop(0, nblk // grp, prep, 0)
        for h in heads:
            rest = kmean_ref[h]
            for part in range(3):
                term = rest.astype(BF16)
                ksplit_ref[h, part] = term
                rest = rest - term.astype(F32)

    blk_id = lax.broadcasted_iota(jnp.int32, (nblk, blk), 0)
    blk_f = blk_id.astype(F32)
    past = blk_id < qi

    def scores(h):
        q = q_ref[:, h * hd:(h + 1) * hd]
        return (_dot_nt(ksplit_ref[h, 0], q) + _dot_nt(ksplit_ref[h, 1], q)) + _dot_nt(ksplit_ref[h, 2], q)

    def select(h, sc):
        q = q_ref[:, h * hd:(h + 1) * hd]
        sc = jnp.where(past, sc, NEG_INF)
        sel = jnp.zeros(sc.shape, jnp.bool_)
        for _ in range(MOBA_TOPK):
            mx = jnp.max(sc, axis=0, keepdims=True)
            first = jnp.min(jnp.where(sc == mx, blk_f, float(nblk)), axis=0, keepdims=True)
            pick = blk_f == first
            sel = sel | (pick & past)
            sc = jnp.where(pick, NEG_INF, sc)
        bias = jnp.where(sel | (blk_id == qi), 0.0, -MASK_BIG)
        pieces = []
        if h > 0:
            pieces.append(jnp.zeros((h * nblk, blk), F32))
        pieces.append(bias)
        if (h + 1) * nblk < hd:
            pieces.append(jnp.zeros((hd - (h + 1) * nblk, blk), F32))
        qaug_ref[h, :, 0:hd] = q
        qaug_ref[h, :, hd:2 * hd] = jnp.concatenate(pieces, axis=0).T.astype(BF16)

    def logits(g, h):
        c0 = pl.multiple_of(g * (grp * blk), grp * blk)
        return _dot_nt(kaug_ref[h, pl.ds(c0, grp * blk), :], qaug_ref[h])

    def softmax(g, h, causal, keys=grp * blk):
        sg = s_ref[h, 0:keys, :]
        if causal:
            kpos = g * (grp * blk) + lax.broadcasted_iota(jnp.int32, sg.shape, 0)
            qpos = qi * blk + lax.broadcasted_iota(jnp.int32, sg.shape, 1)
            sg = jnp.where(kpos <= qpos, sg, -MASK_BIG)
        m_old = m_ref[h]
        m_new = jnp.maximum(m_old, jnp.max(sg, axis=0, keepdims=True))
        alpha_ref[h] = jnp.exp2((m_old - m_new) * c)
        p_ref[h, 0:keys, :] = jnp.exp2(((sg - m_new) * c).astype(BF16))
        m_ref[h] = m_new

    def accumulate(g, h, keys=grp * blk):
        acc_ref[h] = alpha_ref[h] * acc_ref[h] + _dot(vt_ref[h, g, :, 0:keys], p_ref[h, 0:keys, :])

    m_ref[...] = jnp.full(m_ref.shape, -MASK_BIG, F32)
    acc_ref[...] = jnp.zeros(acc_ref.shape, F32)

    last = qi // grp
    sc_all = [scores(h) for h in heads]
    for h in heads:
        select(h, sc_all[h])
        s_ref[h] = logits(0, h)

    def body(g, carry):
        nxt = [logits(g + 1, h) for h in heads]
        for h in heads:
            softmax(g, h, False)
            accumulate(g, h)
        for h in heads:
            s_ref[h] = nxt[h]
        return carry
    lax.fori_loop(0, last, body, 0)
    for own in range(grp):
        @pl.when(qi % grp == own)
        def _():
            for h in heads:
                softmax(last, h, True, keys=(own + 1) * blk)
                accumulate(last, h, keys=(own + 1) * blk)

    for h in heads:
        acc = acc_ref[h]
        o_ref[:, h * hd:(h + 1) * hd] = (acc[0:hd] / acc[hd:hd + 1]).T.astype(o_ref.dtype)


def _moba(qkv, *, batch, seq):
    t = qkv.shape[0]
    nblk = seq // MOBA_BLOCK
    hd = A_HEAD_DIM
    nh = MOBA_HEADS
    hw = nh * hd
    assert nblk % MOBA_GROUP == 0 and nh * nblk <= hd and A_HEADS % nh == 0
    ngrp = A_HEADS // nh
    return pl.pallas_call(
        functools.partial(_moba_kernel, nblk=nblk),
        out_shape=jax.ShapeDtypeStruct((t, A_W), BF16),
        grid=(batch, ngrp, nblk),
        in_specs=[
            pl.BlockSpec((MOBA_BLOCK, hw), lambda b, h, i: (b * nblk + i, h)),
            pl.BlockSpec((seq, hw), lambda b, h, i: (b, ngrp + h), pipeline_mode=pl.Buffered(1)),
            pl.BlockSpec((seq, hw), lambda b, h, i: (b, 2 * ngrp + h), pipeline_mode=pl.Buffered(1)),
        ],
        out_specs=pl.BlockSpec((MOBA_BLOCK, hw), lambda b, h, i: (b * nblk + i, h)),
        scratch_shapes=[
            pltpu.VMEM((nh, seq, 2 * hd), BF16),
            pltpu.VMEM((nh, nblk // MOBA_GROUP, VT_ROWS, MOBA_GROUP * MOBA_BLOCK), BF16),
            pltpu.VMEM((nh, nblk, hd), F32),
            pltpu.VMEM((nh, 3, nblk, hd), BF16),
            pltpu.VMEM((nh, MOBA_BLOCK, 2 * hd), BF16),
            pltpu.VMEM((nh, 1, MOBA_BLOCK), F32),
            pltpu.VMEM((nh, 1, MOBA_BLOCK), F32),
            pltpu.VMEM((nh, VT_ROWS, MOBA_BLOCK), F32),
            pltpu.VMEM((nh, MOBA_GROUP * MOBA_BLOCK, MOBA_BLOCK), F32),
            pltpu.VMEM((nh, MOBA_GROUP * MOBA_BLOCK, MOBA_BLOCK), BF16),
        ],
        compiler_params=_cparams("arbitrary", "arbitrary", "arbitrary"),
        name="moba",
    )(qkv, qkv, qkv)


def _merge_kernel(h_ref, hm_ref, ha_ref, wgm_ref, wga_ref, wm_ref, wa_ref, o_ref):
    h = h_ref[...]
    gm = _sigmoid(_dot(h, wgm_ref[...]))
    ga = _sigmoid(_dot(h, wga_ref[...]))
    um = _dot(hm_ref[...], wm_ref[...])
    ua = _dot(ha_ref[...], wa_ref[...])
    o_ref[...] = (gm * um + ga * ua).astype(o_ref.dtype)


def _merge(h, hm, ha, wgm, wga, wm, wa, *, tm, tn):
    t, d = h.shape
    tm = min(tm, t)
    return pl.pallas_call(
        _merge_kernel,
        out_shape=jax.ShapeDtypeStruct((t, d), BF16),
        grid=(t // tm, d // tn),
        in_specs=[
            pl.BlockSpec((tm, d), lambda i, j: (i, 0)),
            pl.BlockSpec((tm, M_V_W), lambda i, j: (i, 0)),
            pl.BlockSpec((tm, A_W), lambda i, j: (i, 0)),
            pl.BlockSpec((d, tn), lambda i, j: (0, j)),
            pl.BlockSpec((d, tn), lambda i, j: (0, j)),
            pl.BlockSpec((M_V_W, tn), lambda i, j: (0, j)),
            pl.BlockSpec((A_W, tn), lambda i, j: (0, j)),
        ],
        out_specs=pl.BlockSpec((tm, tn), lambda i, j: (i, j)),
        compiler_params=_cparams("arbitrary", "arbitrary"),
        name="merge",
    )(h, hm, ha, wgm, wga, wm, wa)


def _outproj_kernel(x_ref, a_ref, w_ref, g_ref, o_ref, h_ref):
    y = x_ref[...] + _dot(a_ref[...], w_ref[...])
    o_ref[...] = y
    h_ref[...] = _rmsnorm(y, g_ref[...]).astype(BF16)


def _outproj(x, a, w, g, *, tm):
    t, d = x.shape
    k = a.shape[1]
    tm = min(tm, t)
    return pl.pallas_call(
        _outproj_kernel,
        out_shape=(jax.ShapeDtypeStruct((t, d), F32), jax.ShapeDtypeStruct((t, d), BF16)),
        grid=(t // tm,),
        in_specs=[
            pl.BlockSpec((tm, d), lambda i: (i, 0)),
            pl.BlockSpec((tm, k), lambda i: (i, 0)),
            _resident((k, d)),
            _resident((1, d)),
        ],
        out_specs=(pl.BlockSpec((tm, d), lambda i: (i, 0)),
                   pl.BlockSpec((tm, d), lambda i: (i, 0))),
        compiler_params=_cparams("arbitrary"),
        name="outproj",
    )(x, a, w, g)


def _ffn_kernel(x_ref, h_ref, w1_ref, w2_ref, o_ref):
    @pl.when(pl.program_id(1) == 0)
    def _():
        o_ref[...] = x_ref[...]

    u = jnp.maximum(_dot(h_ref[...], w1_ref[...]), 0.0)
    o_ref[...] += _dot((u * u).astype(BF16), w2_ref[...])


def _ffn(x, h, w1, w2, *, tm, tf):
    t, d = x.shape
    dff = w1.shape[1]
    tm = min(tm, t)
    return pl.pallas_call(
        _ffn_kernel,
        out_shape=jax.ShapeDtypeStruct((t, d), F32),
        grid=(t // tm, dff // tf),
        in_specs=[
            pl.BlockSpec((tm, d), lambda i, f: (i, 0)),
            pl.BlockSpec((tm, d), lambda i, f: (i, 0)),
            pl.BlockSpec((d, tf), lambda i, f: (0, f)),
            pl.BlockSpec((tf, d), lambda i, f: (f, 0)),
        ],
        out_specs=pl.BlockSpec((tm, d), lambda i, f: (i, 0)),
        compiler_params=_cparams("arbitrary", "arbitrary"),
        name="ffn",
    )(x, h, w1, w2)


def _ple_kernel(x_ref, p_ref, g_ref, wg_ref, wp_ref, gf_ref, o_ref, *, final):
    x = x_ref[...]
    h = _rmsnorm(x, g_ref[...]).astype(BF16)
    gate = _sigmoid(_dot(h, wg_ref[...]))
    emb = _dot(p_ref[...].astype(BF16), wp_ref[...])
    y = x + gate * emb
    o_ref[...] = _rmsnorm(y, gf_ref[...]) if final else y


def _ple(x, p, g, wg, wp, gf, *, tm, final):
    t, d = x.shape
    tm = min(tm, t)
    return pl.pallas_call(
        functools.partial(_ple_kernel, final=final),
        out_shape=jax.ShapeDtypeStruct((t, d), F32),
        grid=(t // tm,),
        in_specs=[
            pl.BlockSpec((tm, d), lambda i: (i, 0)),
            pl.BlockSpec((tm, PLE_DIM), lambda i: (i, 0)),
            pl.BlockSpec((1, d), lambda i: (0, 0)),
            pl.BlockSpec((d, d), lambda i: (0, 0)),
            pl.BlockSpec((PLE_DIM, d), lambda i: (0, 0)),
            pl.BlockSpec((1, d), lambda i: (0, 0)),
        ],
        out_specs=pl.BlockSpec((tm, d), lambda i: (i, 0)),
        compiler_params=_cparams("arbitrary"),
        name="ple",
    )(x, p, g, wg, wp, gf)


def _layer(x2d, p2d, pos2d, batch, seq, final_norm, final, attn_norm, w_in, b_if, conv_w, conv_b,
           m_out_norm, w_up_m, w_up_a, w_out, mlp_norm, w_ff1, w_ff2, ple_norm, w_ple_gate,
           w_ple_proj):
    d = D_MODEL
    o_if = 2 * M_QK_W + 2 * M_V_W
    o_a = o_if + 2 * M_HEADS
    o_g = o_a + 3 * A_W
    w_gate = jnp.pad(w_in[:, o_if:o_a], ((0, 0), (0, GATE_W - 2 * M_HEADS)))
    w_m = jnp.concatenate([w_in[:, :o_if], w_gate], axis=1).astype(BF16)
    w_a = w_in[:, o_a:o_g].astype(BF16)
    w_gm = w_in[:, o_g:o_g + d].astype(BF16)
    w_ga = w_in[:, o_g + d:].astype(BF16)
    gate_b = jnp.pad(b_if.reshape(1, 2 * M_HEADS), ((0, 0), (0, GATE_W - 2 * M_HEADS)))
    g_attn = attn_norm.reshape(1, d)

    half = ROPE_DIM // 2
    inv_freq = ROPE_THETA ** (-jnp.arange(half, dtype=F32) * 2.0 / ROPE_DIM)
    freq = jnp.concatenate([-inv_freq, inv_freq, jnp.zeros((A_HEAD_DIM - ROPE_DIM,), F32)]).reshape(1, -1)

    zm, h = _inproj_m(x2d, g_attn, w_m, tm=512)
    qkv = _inproj_a(h, w_a, pos2d, freq, tm=512)
    hm = _mlstm(zm, conv_w, conv_b.reshape(1, -1), gate_b, m_out_norm.reshape(1, -1),
                batch=batch, seq=seq, lc=256)
    ha = _moba(qkv, batch=batch, seq=seq)
    merged = _merge(h, hm, ha, w_gm, w_ga, w_up_m.astype(BF16), w_up_a.astype(BF16),
                    tm=1024, tn=512)
    x1, h1 = _outproj(x2d, merged, w_out.astype(BF16), mlp_norm.reshape(1, d), tm=512)
    x2 = _ffn(x1, h1, w_ff1.astype(BF16), w_ff2.astype(BF16), tm=1024, tf=512)
    return _ple(x2, p2d, ple_norm.reshape(1, d), w_ple_gate.astype(BF16), w_ple_proj.astype(BF16),
                final_norm.reshape(1, d), tm=512, final=final)


def kernel(x, p, positions, attn_norm, w_in, b_if, conv_w, conv_b, m_out_norm, w_up_m, w_up_a,
           w_out, mlp_norm, w_ff1, w_ff2, ple_norm, w_ple_gate, w_ple_proj, final_norm):
    batch, seq, d = x.shape
    depth = w_in.shape[0]
    t = batch * seq
    x2d = x.reshape(t, d)
    pos2d = positions.reshape(t, 1)
    for i in range(depth):
        x2d = _layer(
            x2d, p[i].reshape(t, PLE_DIM), pos2d, batch, seq, final_norm, i == depth - 1,
            attn_norm[i], w_in[i], b_if[i], conv_w[i], conv_b[i], m_out_norm[i], w_up_m[i],
            w_up_a[i], w_out[i], mlp_norm[i], w_ff1[i], w_ff2[i], ple_norm[i], w_ple_gate[i],
            w_ple_proj[i])
    return x2d.reshape(batch, seq, d)
```

```python
import functools

import jax
import jax.numpy as jnp
from jax import lax
from jax.experimental import pallas as pl
from jax.experimental.pallas import tpu as pltpu

D_MODEL = 2048
PLE_DIM = 256
M_HEADS = 4
M_QK_DIM = 128
M_V_DIM = 256
M_CONV = 4
A_HEADS = 8
A_HEAD_DIM = 128
MOBA_BLOCK = 256
MOBA_TOPK = 3
ROPE_THETA = 500000.0
ROPE_DIM = A_HEAD_DIM // 4
D_FF = 4 * D_MODEL
EPS = 1e-6

M_QK_W = M_HEADS * M_QK_DIM
M_V_W = M_HEADS * M_V_DIM
A_W = A_HEADS * A_HEAD_DIM

LANES = 128
SUBLANES = 8
BF16_SUBLANES = 16
GATE_W = LANES
ZM_W = 2 * M_QK_W + 2 * M_V_W + GATE_W
VMEM_LIMIT = 56 * 1024 * 1024

TM_RESIDENT = 512
TM_RESIDENT_BF16 = 1024
TM_STREAMED = 1024
MERGE_TN = 512
FFN_TF = 512
MLSTM_CHUNK = 256

F32 = jnp.float32
BF16 = jnp.bfloat16
NEG_INF = float("-inf")
LOG2_E = 1.4426950408889634


def _cparams(*sem):
    return pltpu.CompilerParams(dimension_semantics=sem, vmem_limit_bytes=VMEM_LIMIT)


def _rmsnorm(x, g):
    ms = jnp.mean(x * x, axis=-1, keepdims=True)
    return (x * lax.rsqrt(ms + EPS)) * g


def _sigmoid(x):
    return 1.0 / (1.0 + jnp.exp(-x))


def _dot(a, b):
    return jnp.dot(a, b, preferred_element_type=F32)


def _dot_nt(a, b):
    return lax.dot_general(a, b, (((1,), (1,)), ((), ())), preferred_element_type=F32)


def _resident(shape):
    return pl.BlockSpec(shape, lambda *_: (0,) * len(shape), pipeline_mode=pl.Buffered(1))


def _inproj_m_kernel(x_ref, g_ref, w_ref, wg_ref, o_ref, h_ref):
    h = _rmsnorm(x_ref[...], g_ref[...]).astype(BF16)
    h_ref[...] = h
    n = w_ref.shape[1]
    o_ref[:, 0:n] = _dot(h, w_ref[...])
    o_ref[:, n:] = _dot(h, wg_ref[...])


def _inproj_m(x, g, w, w_gate, *, tm):
    t, d = x.shape
    n = w.shape[1]
    ng = w_gate.shape[1]
    tm = min(tm, t)
    return pl.pallas_call(
        _inproj_m_kernel,
        out_shape=(jax.ShapeDtypeStruct((t, n + ng), F32), jax.ShapeDtypeStruct((t, d), BF16)),
        grid=(t // tm,),
        in_specs=[
            pl.BlockSpec((tm, d), lambda i: (i, 0)),
            _resident((1, d)),
            _resident((d, n)),
            _resident((d, ng)),
        ],
        out_specs=(pl.BlockSpec((tm, n + ng), lambda i: (i, 0)),
                   pl.BlockSpec((tm, d), lambda i: (i, 0))),
        compiler_params=_cparams("arbitrary"),
        name="inproj_m",
    )(x, g, w, w_gate)


def _inproj_a_kernel(h_ref, w_ref, pos_ref, freq_ref, o_ref):
    h = h_ref[...]
    ang = pos_ref[...].astype(F32) * freq_ref[...]
    c = jnp.cos(ang)
    s = jnp.sin(ang)
    lane = lax.broadcasted_iota(jnp.int32, c.shape, 1)
    half = ROPE_DIM // 2
    for part in range(2):
        z = _dot(h, w_ref[:, part * A_W:(part + 1) * A_W])
        for hd in range(A_HEADS):
            zh = z[:, hd * A_HEAD_DIM:(hd + 1) * A_HEAD_DIM]
            partner = jnp.where(lane < half,
                                pltpu.roll(zh, A_HEAD_DIM - half, 1),
                                pltpu.roll(zh, half, 1))
            rot = jnp.where(lane < ROPE_DIM, zh * c + partner * s, zh)
            c0 = part * A_W + hd * A_HEAD_DIM
            o_ref[:, c0:c0 + A_HEAD_DIM] = rot.astype(o_ref.dtype)
    o_ref[:, 2 * A_W:] = _dot(h, w_ref[:, 2 * A_W:]).astype(o_ref.dtype)


def _inproj_a(h, w, pos, freq, *, tm):
    t, d = h.shape
    n = w.shape[1]
    tm = min(tm, t)
    return pl.pallas_call(
        _inproj_a_kernel,
        out_shape=jax.ShapeDtypeStruct((t, n), BF16),
        grid=(t // tm,),
        in_specs=[
            pl.BlockSpec((tm, d), lambda i: (i, 0)),
            _resident((d, n)),
            pl.BlockSpec((tm, 1), lambda i: (i, 0)),
            _resident((1, A_HEAD_DIM)),
        ],
        out_specs=pl.BlockSpec((tm, n), lambda i: (i, 0)),
        compiler_params=_cparams("arbitrary"),
        name="inproj_a",
    )(h, w, pos, freq)


def _mlstm_kernel(qk_ref, v_ref, o_ref, gate_ref, cw_ref, cb_ref, gb_ref, gn_ref, out_ref,
                  buf_ref, c_ref, n_ref, m_ref, *, lc):
    ci = pl.program_id(1)
    pad = SUBLANES

    @pl.when(ci == 0)
    def _():
        buf_ref[0:pad, :] = jnp.zeros((pad, 2 * M_QK_W), F32)
        c_ref[...] = jnp.zeros(c_ref.shape, F32)
        n_ref[...] = jnp.zeros(n_ref.shape, F32)
        m_ref[...] = jnp.zeros(m_ref.shape, F32)

    buf_ref[pad:pad + lc, :] = qk_ref[...]
    y = cb_ref[...] + cw_ref[M_CONV - 1:M_CONV, :] * buf_ref[pad:pad + lc, :]
    for j in range(1, M_CONV):
        y = y + cw_ref[M_CONV - 1 - j:M_CONV - j, :] * buf_ref[pad - j:pad - j + lc, :]
    buf_ref[0:pad, :] = buf_ref[lc:lc + pad, :]
    qk = y * _sigmoid(y)

    gates = gate_ref[...] + gb_ref[...]
    lf = jnp.minimum(gates, 0.0) - jnp.log1p(jnp.exp(-jnp.abs(gates)))
    row = lax.broadcasted_iota(jnp.int32, (lc, lc), 0)
    col = lax.broadcasted_iota(jnp.int32, (lc, lc), 1)
    causal = row >= col
    tri = causal.astype(F32)
    bcum = jnp.dot(tri, lf, preferred_element_type=F32, precision=lax.Precision.HIGHEST)
    gates_t = gates.T
    bcum_t = bcum.T

    heads = range(M_HEADS)
    q_bf, k_f32, k_bf, v_bf, qk_dot, qc_dot, qn_sum = [], [], [], [], [], [], []
    for hd in heads:
        q = qk[:, hd * M_QK_DIM:(hd + 1) * M_QK_DIM]
        k = qk[:, M_QK_W + hd * M_QK_DIM:M_QK_W + (hd + 1) * M_QK_DIM] * (M_QK_DIM ** -0.5)
        q_bf.append(q.astype(BF16))
        k_f32.append(k)
        k_bf.append(k.astype(BF16))
        v_bf.append(v_ref[:, hd * M_V_DIM:(hd + 1) * M_V_DIM].astype(BF16))
        qk_dot.append(_dot_nt(q_bf[hd], k_bf[hd]))
        qc_dot.append(_dot(q_bf[hd], c_ref[hd].astype(BF16)))
        qn_sum.append(jnp.sum(q * n_ref[hd], axis=-1, keepdims=True))

    for hd in heads:
        ig_col = gates[:, hd:hd + 1]
        ig_row = gates_t[hd:hd + 1, :]
        b_col = bcum[:, M_HEADS + hd:M_HEADS + hd + 1]
        b_row = bcum_t[M_HEADS + hd:M_HEADS + hd + 1, :]
        m_prev = m_ref[hd][:, 0:1]

        dmat = jnp.where(causal, b_col - b_row + ig_row, NEG_INF)
        inter = b_col + m_prev
        m_t = jnp.maximum(inter, jnp.max(dmat, axis=-1, keepdims=True))
        s = qk_dot[hd] * jnp.exp(dmat - m_t)
        e_inter = jnp.exp(inter - m_t)
        num = e_inter * qc_dot[hd] + _dot(s.astype(BF16), v_bf[hd])
        den = e_inter * qn_sum[hd] + jnp.sum(s, axis=-1, keepdims=True)
        h = num / jnp.maximum(jnp.abs(den), jnp.exp(-m_t))

        hn = h * lax.rsqrt(jnp.mean(h * h, axis=-1, keepdims=True) + EPS)
        hn = hn * gn_ref[:, hd * M_V_DIM:(hd + 1) * M_V_DIM]
        og = _sigmoid(o_ref[:, hd * M_V_DIM:(hd + 1) * M_V_DIM])
        out_ref[:, hd * M_V_DIM:(hd + 1) * M_V_DIM] = (hn * og).astype(out_ref.dtype)

    for hd in heads:
        ig_col = gates[:, hd:hd + 1]
        b_col = bcum[:, M_HEADS + hd:M_HEADS + hd + 1]
        m_prev = m_ref[hd][:, 0:1]
        g_last = b_col[lc - 1:lc, :]
        a_col = g_last - b_col + ig_col
        m_new = jnp.maximum(g_last + m_prev, jnp.max(a_col, axis=0, keepdims=True))
        w_col = jnp.exp(a_col - m_new)
        decay = jnp.exp(g_last + m_prev - m_new)
        kw = k_f32[hd] * w_col
        c_ref[hd] = decay * c_ref[hd] + _dot(kw.T.astype(BF16), v_bf[hd])
        n_ref[hd] = decay * n_ref[hd] + jnp.sum(kw, axis=0, keepdims=True)
        m_ref[hd] = jnp.broadcast_to(m_new, (1, LANES))


def _mlstm(zm, conv_w, conv_b, gate_b, out_norm, *, batch, seq, lc):
    t = zm.shape[0]
    nc = seq // lc
    qkw = 2 * M_QK_W
    return pl.pallas_call(
        functools.partial(_mlstm_kernel, lc=lc),
        out_shape=jax.ShapeDtypeStruct((t, M_V_W), BF16),
        grid=(batch, nc),
        in_specs=[
            pl.BlockSpec((lc, qkw), lambda b, c: (b * nc + c, 0)),
            pl.BlockSpec((lc, M_V_W), lambda b, c: (b * nc + c, qkw // M_V_W)),
            pl.BlockSpec((lc, M_V_W), lambda b, c: (b * nc + c, qkw // M_V_W + 1)),
            pl.BlockSpec((lc, GATE_W), lambda b, c: (b * nc + c, (qkw + 2 * M_V_W) // GATE_W)),
            pl.BlockSpec((M_CONV, qkw), lambda b, c: (0, 0)),
            pl.BlockSpec((1, qkw), lambda b, c: (0, 0)),
            pl.BlockSpec((1, GATE_W), lambda b, c: (0, 0)),
            pl.BlockSpec((1, M_V_W), lambda b, c: (0, 0)),
        ],
        out_specs=pl.BlockSpec((lc, M_V_W), lambda b, c: (b * nc + c, 0)),
        scratch_shapes=[
            pltpu.VMEM((lc + SUBLANES, qkw), F32),
            pltpu.VMEM((M_HEADS, M_QK_DIM, M_V_DIM), F32),
            pltpu.VMEM((M_HEADS, 1, M_QK_DIM), F32),
            pltpu.VMEM((M_HEADS, 1, LANES), F32),
        ],
        compiler_params=_cparams("arbitrary", "arbitrary"),
        name="mlstm",
    )(zm, zm, zm, zm, conv_w, conv_b, gate_b, out_norm)


MOBA_GROUP = 4
MOBA_HEADS = 4
MASK_BIG = 1e30
VT_ROWS = A_HEAD_DIM + BF16_SUBLANES


def _moba_kernel(q_ref, k_ref, v_ref, o_ref, kaug_ref, vt_ref, kmean_ref, ksplit_ref, qaug_ref,
                 m_ref, alpha_ref, acc_ref, s_ref, p_ref, *, nblk):
    qi = pl.program_id(2)
    blk = MOBA_BLOCK
    hd = A_HEAD_DIM
    grp = MOBA_GROUP
    nh = MOBA_HEADS
    heads = range(nh)
    c = (A_HEAD_DIM ** -0.5) * LOG2_E

    @pl.when(qi == 0)
    def _():
        lane = lax.broadcasted_iota(jnp.int32, (blk, hd), 1)
        row = lax.broadcasted_iota(jnp.int32, (VT_ROWS - hd, grp * blk), 0)
        ones_rows = (row == 0).astype(BF16)

        def prep(g, carry):
            for h in heads:
                vt_ref[h, g, hd:VT_ROWS, :] = ones_rows
            for gg in range(grp):
                j = g * grp + gg
                r0 = pl.multiple_of(j * blk, blk)
                for h in heads:
                    kj = k_ref[pl.ds(r0, blk), h * hd:(h + 1) * hd]
                    kaug_ref[h, pl.ds(r0, blk), 0:hd] = kj
                    kaug_ref[h, pl.ds(r0, blk), hd:2 * hd] = (lane == nblk * h + j).astype(BF16)
                    kmean_ref[h, pl.ds(j, 1), :] = jnp.mean(kj.astype(F32), axis=0, keepdims=True)
                    vt_ref[h, g, 0:hd, gg * blk:(gg + 1) * blk] = (
                        v_ref[pl.ds(r0, blk), h * hd:(h + 1) * hd].astype(F32).T.astype(BF16))
            return carry
        lax.fori_loop(0, nblk // grp, prep, 0)
        for h in heads:
            rest = kmean_ref[h]
            for part in range(3):
                term = rest.astype(BF16)
                ksplit_ref[h, part] = term
                rest = rest - term.astype(F32)

    blk_id = lax.broadcasted_iota(jnp.int32, (nblk, blk), 0)
    blk_f = blk_id.astype(F32)
    past = blk_id < qi

    def scores(h):
        q = q_ref[:, h * hd:(h + 1) * hd]
        return (_dot_nt(ksplit_ref[h, 0], q) + _dot_nt(ksplit_ref[h, 1], q)) + _dot_nt(ksplit_ref[h, 2], q)

    def select(h, sc):
        q = q_ref[:, h * hd:(h + 1) * hd]
        sc = jnp.where(past, sc, NEG_INF)
        bias = jnp.where(blk_id == qi, 0.0, -MASK_BIG)
        for _ in range(MOBA_TOPK):
            mx = jnp.max(sc, axis=0, keepdims=True)
            first = jnp.min(jnp.where(sc == mx, blk_f, float(nblk)), axis=0, keepdims=True)
            first = jnp.where(mx > NEG_INF, first, -1.0)
            pick = blk_f == first
            bias = jnp.where(pick, 0.0, bias)
            sc = jnp.where(pick, NEG_INF, sc)
        pieces = []
        if h > 0:
            pieces.append(jnp.zeros((h * nblk, blk), F32))
        pieces.append(bias)
        if (h + 1) * nblk < hd:
            pieces.append(jnp.zeros((hd - (h + 1) * nblk, blk), F32))
        qaug_ref[h, :, 0:hd] = q
        qaug_ref[h, :, hd:2 * hd] = jnp.concatenate(pieces, axis=0).T.astype(BF16)

    def logits(g, h):
        c0 = pl.multiple_of(g * (grp * blk), grp * blk)
        return _dot_nt(kaug_ref[h, pl.ds(c0, grp * blk), :], qaug_ref[h])

    def softmax(g, h, causal, keys=grp * blk):
        sg = s_ref[h, 0:keys, :]
        if causal:
            kpos = g * (grp * blk) + lax.broadcasted_iota(jnp.int32, sg.shape, 0)
            qpos = qi * blk + lax.broadcasted_iota(jnp.int32, sg.shape, 1)
            sg = jnp.where(kpos <= qpos, sg, -MASK_BIG)
        m_old = m_ref[h]
        m_new = jnp.maximum(m_old, jnp.max(sg, axis=0, keepdims=True))
        alpha_ref[h] = jnp.exp2((m_old - m_new) * c)
        p_ref[h, 0:keys, :] = jnp.exp2(((sg - m_new) * c).astype(BF16))
        m_ref[h] = m_new

    def accumulate(g, h, keys=grp * blk):
        acc_ref[h] = alpha_ref[h] * acc_ref[h] + _dot(vt_ref[h, g, :, 0:keys], p_ref[h, 0:keys, :])

    m_ref[...] = jnp.full(m_ref.shape, -MASK_BIG, F32)
    acc_ref[...] = jnp.zeros(acc_ref.shape, F32)

    last = qi // grp
    sc_all = [scores(h) for h in heads]
    for h in heads:
        select(h, sc_all[h])
        s_ref[h] = logits(0, h)

    def body(g, carry):
        nxt = [logits(g + 1, h) for h in heads]
        for h in heads:
            softmax(g, h, False)
            accumulate(g, h)
        for h in heads:
            s_ref[h] = nxt[h]
        return carry
    lax.fori_loop(0, last, body, 0)
    for own in range(grp):
        @pl.when(qi % grp == own)
        def _():
            for h in heads:
                softmax(last, h, True, keys=(own + 1) * blk)
                accumulate(last, h, keys=(own + 1) * blk)

    for h in heads:
        acc = acc_ref[h]
        o_ref[:, h * hd:(h + 1) * hd] = (acc[0:hd] / acc[hd:hd + 1]).T.astype(o_ref.dtype)


def _moba(qkv, *, batch, seq):
    t = qkv.shape[0]
    nblk = seq // MOBA_BLOCK
    hd = A_HEAD_DIM
    nh = MOBA_HEADS
    hw = nh * hd
    assert nblk % MOBA_GROUP == 0 and nh * nblk <= hd and A_HEADS % nh == 0
    ngrp = A_HEADS // nh
    return pl.pallas_call(
        functools.partial(_moba_kernel, nblk=nblk),
        out_shape=jax.ShapeDtypeStruct((t, A_W), BF16),
        grid=(batch, ngrp, nblk),
        in_specs=[
            pl.BlockSpec((MOBA_BLOCK, hw), lambda b, h, i: (b * nblk + i, h)),
            pl.BlockSpec((seq, hw), lambda b, h, i: (b, ngrp + h), pipeline_mode=pl.Buffered(1)),
            pl.BlockSpec((seq, hw), lambda b, h, i: (b, 2 * ngrp + h), pipeline_mode=pl.Buffered(1)),
        ],
        out_specs=pl.BlockSpec((MOBA_BLOCK, hw), lambda b, h, i: (b * nblk + i, h)),
        scratch_shapes=[
            pltpu.VMEM((nh, seq, 2 * hd), BF16),
            pltpu.VMEM((nh, nblk // MOBA_GROUP, VT_ROWS, MOBA_GROUP * MOBA_BLOCK), BF16),
            pltpu.VMEM((nh, nblk, hd), F32),
            pltpu.VMEM((nh, 3, nblk, hd), BF16),
            pltpu.VMEM((nh, MOBA_BLOCK, 2 * hd), BF16),
            pltpu.VMEM((nh, 1, MOBA_BLOCK), F32),
            pltpu.VMEM((nh, 1, MOBA_BLOCK), F32),
            pltpu.VMEM((nh, VT_ROWS, MOBA_BLOCK), F32),
            pltpu.VMEM((nh, MOBA_GROUP * MOBA_BLOCK, MOBA_BLOCK), F32),
            pltpu.VMEM((nh, MOBA_GROUP * MOBA_BLOCK, MOBA_BLOCK), BF16),
        ],
        compiler_params=_cparams("arbitrary", "arbitrary", "arbitrary"),
        name="moba",
    )(qkv, qkv, qkv)


def _merge_kernel(h_ref, hm_ref, ha_ref, wgm_ref, wga_ref, wm_ref, wa_ref, o_ref):
    h = h_ref[...]
    gm = _sigmoid(_dot(h, wgm_ref[...]))
    ga = _sigmoid(_dot(h, wga_ref[...]))
    um = _dot(hm_ref[...], wm_ref[...])
    ua = _dot(ha_ref[...], wa_ref[...])
    o_ref[...] = (gm * um + ga * ua).astype(o_ref.dtype)


def _merge(h, hm, ha, wgm, wga, wm, wa, *, tm, tn):
    t, d = h.shape
    tm = min(tm, t)
    return pl.pallas_call(
        _merge_kernel,
        out_shape=jax.ShapeDtypeStruct((t, d), BF16),
        grid=(t // tm, d // tn),
        in_specs=[
            pl.BlockSpec((tm, d), lambda i, j: (i, 0)),
            pl.BlockSpec((tm, M_V_W), lambda i, j: (i, 0)),
            pl.BlockSpec((tm, A_W), lambda i, j: (i, 0)),
            pl.BlockSpec((d, tn), lambda i, j: (0, j)),
            pl.BlockSpec((d, tn), lambda i, j: (0, j)),
            pl.BlockSpec((M_V_W, tn), lambda i, j: (0, j)),
            pl.BlockSpec((A_W, tn), lambda i, j: (0, j)),
        ],
        out_specs=pl.BlockSpec((tm, tn), lambda i, j: (i, j)),
        compiler_params=_cparams("arbitrary", "arbitrary"),
        name="merge",
    )(h, hm, ha, wgm, wga, wm, wa)


def _outproj_kernel(x_ref, a_ref, w_ref, g_ref, o_ref, h_ref):
    y = x_ref[...] + _dot(a_ref[...], w_ref[...])
    o_ref[...] = y
    h_ref[...] = _rmsnorm(y, g_ref[...]).astype(BF16)


def _outproj(x, a, w, g, *, tm):
    t, d = x.shape
    k = a.shape[1]
    tm = min(tm, t)
    return pl.pallas_call(
        _outproj_kernel,
        out_shape=(jax.ShapeDtypeStruct((t, d), F32), jax.ShapeDtypeStruct((t, d), BF16)),
        grid=(t // tm,),
        in_specs=[
            pl.BlockSpec((tm, d), lambda i: (i, 0)),
            pl.BlockSpec((tm, k), lambda i: (i, 0)),
            _resident((k, d)),
            _resident((1, d)),
        ],
        out_specs=(pl.BlockSpec((tm, d), lambda i: (i, 0)),
                   pl.BlockSpec((tm, d), lambda i: (i, 0))),
        compiler_params=_cparams("arbitrary"),
        name="outproj",
    )(x, a, w, g)


def _ffn_kernel(x_ref, h_ref, w1_ref, w2_ref, o_ref):
    @pl.when(pl.program_id(1) == 0)
    def _():
        o_ref[...] = x_ref[...]

    u = jnp.maximum(_dot(h_ref[...], w1_ref[...]), 0.0)
    o_ref[...] += _dot((u * u).astype(BF16), w2_ref[...])


def _ffn(x, h, w1, w2, *, tm, tf):
    t, d = x.shape
    dff = w1.shape[1]
    tm = min(tm, t)
    return pl.pallas_call(
        _ffn_kernel,
        out_shape=jax.ShapeDtypeStruct((t, d), F32),
        grid=(t // tm, dff // tf),
        in_specs=[
            pl.BlockSpec((tm, d), lambda i, f: (i, 0)),
            pl.BlockSpec((tm, d), lambda i, f: (i, 0)),
            pl.BlockSpec((d, tf), lambda i, f: (0, f)),
            pl.BlockSpec((tf, d), lambda i, f: (f, 0)),
        ],
        out_specs=pl.BlockSpec((tm, d), lambda i, f: (i, 0)),
        compiler_params=_cparams("arbitrary", "arbitrary"),
        name="ffn",
    )(x, h, w1, w2)


def _ple_kernel(x_ref, p_ref, g_ref, wg_ref, wp_ref, gf_ref, o_ref, *, final):
    x = x_ref[...]
    h = _rmsnorm(x, g_ref[...]).astype(BF16)
    gate = _sigmoid(_dot(h, wg_ref[...]))
    emb = _dot(p_ref[...].astype(BF16), wp_ref[...])
    y = x + gate * emb
    o_ref[...] = _rmsnorm(y, gf_ref[...]) if final else y


def _ple(x, p, g, wg, wp, gf, *, tm, final):
    t, d = x.shape
    tm = min(tm, t)
    return pl.pallas_call(
        functools.partial(_ple_kernel, final=final),
        out_shape=jax.ShapeDtypeStruct((t, d), F32),
        grid=(t // tm,),
        in_specs=[
            pl.BlockSpec((tm, d), lambda i: (i, 0)),
            pl.BlockSpec((tm, PLE_DIM), lambda i: (i, 0)),
            _resident((1, d)),
            _resident((d, d)),
            _resident((PLE_DIM, d)),
            _resident((1, d)),
        ],
        out_specs=pl.BlockSpec((tm, d), lambda i: (i, 0)),
        compiler_params=_cparams("arbitrary"),
        name="ple",
    )(x, p, g, wg, wp, gf)


def _layer(x2d, p2d, pos2d, batch, seq, final_norm, final, attn_norm, w_in, b_if, conv_w, conv_b,
           m_out_norm, w_up_m, w_up_a, w_out, mlp_norm, w_ff1, w_ff2, ple_norm, w_ple_gate,
           w_ple_proj):
    d = D_MODEL
    o_if = 2 * M_QK_W + 2 * M_V_W
    o_a = o_if + 2 * M_HEADS
    o_g = o_a + 3 * A_W
    w_gate = jnp.pad(w_in[:, o_if:o_a], ((0, 0), (0, GATE_W - 2 * M_HEADS))).astype(BF16)
    w_m = w_in[:, :o_if].astype(BF16)
    w_a = w_in[:, o_a:o_g].astype(BF16)
    w_gm = w_in[:, o_g:o_g + d].astype(BF16)
    w_ga = w_in[:, o_g + d:].astype(BF16)
    gate_b = jnp.pad(b_if.reshape(1, 2 * M_HEADS), ((0, 0), (0, GATE_W - 2 * M_HEADS)))
    g_attn = attn_norm.reshape(1, d)

    half = ROPE_DIM // 2
    inv_freq = ROPE_THETA ** (-jnp.arange(half, dtype=F32) * 2.0 / ROPE_DIM)
    freq = jnp.concatenate([-inv_freq, inv_freq, jnp.zeros((A_HEAD_DIM - ROPE_DIM,), F32)]).reshape(1, -1)

    zm, h = _inproj_m(x2d, g_attn, w_m, w_gate, tm=TM_RESIDENT)
    qkv = _inproj_a(h, w_a, pos2d, freq, tm=TM_RESIDENT_BF16)
    hm = _mlstm(zm, conv_w, conv_b.reshape(1, -1), gate_b, m_out_norm.reshape(1, -1),
                batch=batch, seq=seq, lc=MLSTM_CHUNK)
    ha = _moba(qkv, batch=batch, seq=seq)
    merged = _merge(h, hm, ha, w_gm, w_ga, w_up_m.astype(BF16), w_up_a.astype(BF16),
                    tm=TM_STREAMED, tn=MERGE_TN)
    x1, h1 = _outproj(x2d, merged, w_out.astype(BF16), mlp_norm.reshape(1, d), tm=TM_RESIDENT)
    x2 = _ffn(x1, h1, w_ff1.astype(BF16), w_ff2.astype(BF16), tm=TM_STREAMED, tf=FFN_TF)
    return _ple(x2, p2d, ple_norm.reshape(1, d), w_ple_gate.astype(BF16), w_ple_proj.astype(BF16),
                final_norm.reshape(1, d), tm=TM_RESIDENT, final=final)


def kernel(x, p, positions, attn_norm, w_in, b_if, conv_w, conv_b, m_out_norm, w_up_m, w_up_a,
           w_out, mlp_norm, w_ff1, w_ff2, ple_norm, w_ple_gate, w_ple_proj, final_norm):
    batch, seq, d = x.shape
    depth = w_in.shape[0]
    t = batch * seq
    x2d = x.reshape(t, d)
    pos2d = positions.reshape(t, 1)
    for i in range(depth):
        x2d = _layer(
            x2d, p[i].reshape(t, PLE_DIM), pos2d, batch, seq, final_norm, i == depth - 1,
            attn_norm[i], w_in[i], b_if[i], conv_w[i], conv_b[i], m_out_norm[i], w_up_m[i],
            w_up_a[i], w_out[i], mlp_norm[i], w_ff1[i], w_ff2[i], ple_norm[i], w_ple_gate[i],
            w_ple_proj[i])
    return x2d.reshape(batch, seq, d)
```

```python
import functools

import jax
import jax.numpy as jnp
from jax import lax
from jax.experimental import pallas as pl
from jax.experimental.pallas import tpu as pltpu

D_MODEL = 2048
PLE_DIM = 256
M_HEADS = 4
M_QK_DIM = 128
M_V_DIM = 256
M_CONV = 4
A_HEADS = 8
A_HEAD_DIM = 128
MOBA_BLOCK = 256
MOBA_TOPK = 3
ROPE_THETA = 500000.0
ROPE_DIM = A_HEAD_DIM // 4
D_FF = 4 * D_MODEL
EPS = 1e-6

M_QK_W = M_HEADS * M_QK_DIM
M_V_W = M_HEADS * M_V_DIM
A_W = A_HEADS * A_HEAD_DIM

LANES = 128
SUBLANES = 8
BF16_SUBLANES = 16
GATE_W = LANES
ZM_W = 2 * M_QK_W + 2 * M_V_W + GATE_W
VMEM_LIMIT = 56 * 1024 * 1024

TM_RESIDENT = 512
TM_RESIDENT_BF16 = 1024
TM_STREAMED = 1024
MERGE_TN = 512
FFN_TF = 1024
MLSTM_CHUNK = 256

F32 = jnp.float32
BF16 = jnp.bfloat16
NEG_INF = float("-inf")
LOG2_E = 1.4426950408889634


def _cparams(*sem):
    return pltpu.CompilerParams(dimension_semantics=sem, vmem_limit_bytes=VMEM_LIMIT)


def _rmsnorm(x, g):
    ms = jnp.mean(x * x, axis=-1, keepdims=True)
    return (x * lax.rsqrt(ms + EPS)) * g


def _sigmoid(x):
    return 1.0 / (1.0 + jnp.exp(-x))


def _dot(a, b):
    return jnp.dot(a, b, preferred_element_type=F32)


def _dot_nt(a, b):
    return lax.dot_general(a, b, (((1,), (1,)), ((), ())), preferred_element_type=F32)


def _resident(shape):
    return pl.BlockSpec(shape, lambda *_: (0,) * len(shape), pipeline_mode=pl.Buffered(1))


def _inproj_m_kernel(x_ref, g_ref, w_ref, wg_ref, o_ref, h_ref):
    h = _rmsnorm(x_ref[...], g_ref[...]).astype(BF16)
    h_ref[...] = h
    n = w_ref.shape[1]
    o_ref[:, 0:n] = _dot(h, w_ref[...])
    o_ref[:, n:] = _dot(h, wg_ref[...])


def _inproj_m(x, g, w, w_gate, *, tm):
    t, d = x.shape
    n = w.shape[1]
    ng = w_gate.shape[1]
    tm = min(tm, t)
    return pl.pallas_call(
        _inproj_m_kernel,
        out_shape=(jax.ShapeDtypeStruct((t, n + ng), F32), jax.ShapeDtypeStruct((t, d), BF16)),
        grid=(t // tm,),
        in_specs=[
            pl.BlockSpec((tm, d), lambda i: (i, 0)),
            _resident((1, d)),
            _resident((d, n)),
            _resident((d, ng)),
        ],
        out_specs=(pl.BlockSpec((tm, n + ng), lambda i: (i, 0)),
                   pl.BlockSpec((tm, d), lambda i: (i, 0))),
        compiler_params=_cparams("arbitrary"),
        name="inproj_m",
    )(x, g, w, w_gate)


def _inproj_a_kernel(h_ref, w_ref, pos_ref, freq_ref, o_ref):
    h = h_ref[...]
    ang = pos_ref[...].astype(F32) * freq_ref[...]
    c = jnp.cos(ang)
    s = jnp.sin(ang)
    lane = lax.broadcasted_iota(jnp.int32, c.shape, 1)
    half = ROPE_DIM // 2
    for part in range(2):
        z = _dot(h, w_ref[:, part * A_W:(part + 1) * A_W])
        for hd in range(A_HEADS):
            zh = z[:, hd * A_HEAD_DIM:(hd + 1) * A_HEAD_DIM]
            partner = jnp.where(lane < half,
                                pltpu.roll(zh, A_HEAD_DIM - half, 1),
                                pltpu.roll(zh, half, 1))
            rot = jnp.where(lane < ROPE_DIM, zh * c + partner * s, zh)
            c0 = part * A_W + hd * A_HEAD_DIM
            o_ref[:, c0:c0 + A_HEAD_DIM] = rot.astype(o_ref.dtype)
    o_ref[:, 2 * A_W:] = _dot(h, w_ref[:, 2 * A_W:]).astype(o_ref.dtype)


def _inproj_a(h, w, pos, freq, *, tm):
    t, d = h.shape
    n = w.shape[1]
    tm = min(tm, t)
    return pl.pallas_call(
        _inproj_a_kernel,
        out_shape=jax.ShapeDtypeStruct((t, n), BF16),
        grid=(t // tm,),
        in_specs=[
            pl.BlockSpec((tm, d), lambda i: (i, 0)),
            _resident((d, n)),
            pl.BlockSpec((tm, 1), lambda i: (i, 0)),
            _resident((1, A_HEAD_DIM)),
        ],
        out_specs=pl.BlockSpec((tm, n), lambda i: (i, 0)),
        compiler_params=_cparams("arbitrary"),
        name="inproj_a",
    )(h, w, pos, freq)


def _mlstm_kernel(qk_ref, v_ref, o_ref, gate_ref, cw_ref, cb_ref, gb_ref, gn_ref, out_ref,
                  buf_ref, c_ref, n_ref, m_ref, *, lc):
    ci = pl.program_id(1)
    pad = SUBLANES

    @pl.when(ci == 0)
    def _():
        buf_ref[0:pad, :] = jnp.zeros((pad, 2 * M_QK_W), F32)
        c_ref[...] = jnp.zeros(c_ref.shape, F32)
        n_ref[...] = jnp.zeros(n_ref.shape, F32)
        m_ref[...] = jnp.zeros(m_ref.shape, F32)

    def conv(window):
        y = cb_ref[...] + cw_ref[M_CONV - 1:M_CONV, :] * window(0)
        for j in range(1, M_CONV):
            y = y + cw_ref[M_CONV - 1 - j:M_CONV - j, :] * window(j)
        return y

    buf_ref[pad:2 * pad, :] = qk_ref[0:pad, :]
    y_head = conv(lambda j: buf_ref[pad - j:2 * pad - j, :])
    y_body = conv(lambda j: qk_ref[pad - j:lc - j, :])
    buf_ref[0:pad, :] = qk_ref[lc - pad:lc, :]
    y = jnp.concatenate([y_head, y_body], axis=0)
    qk = y * _sigmoid(y)

    gates = gate_ref[...] + gb_ref[...]
    lf = jnp.minimum(gates, 0.0) - jnp.log1p(jnp.exp(-jnp.abs(gates)))
    row = lax.broadcasted_iota(jnp.int32, (lc, lc), 0)
    col = lax.broadcasted_iota(jnp.int32, (lc, lc), 1)
    causal = row >= col
    tri = causal.astype(F32)
    bcum = jnp.dot(tri, lf, preferred_element_type=F32, precision=lax.Precision.HIGHEST)
    gates_t = gates.T
    bcum_t = bcum.T

    heads = range(M_HEADS)
    q_bf, k_f32, k_bf, v_bf, qk_dot, qc_dot, qn_sum = [], [], [], [], [], [], []
    for hd in heads:
        q = qk[:, hd * M_QK_DIM:(hd + 1) * M_QK_DIM]
        k = qk[:, M_QK_W + hd * M_QK_DIM:M_QK_W + (hd + 1) * M_QK_DIM] * (M_QK_DIM ** -0.5)
        q_bf.append(q.astype(BF16))
        k_f32.append(k)
        k_bf.append(k.astype(BF16))
        v_bf.append(v_ref[:, hd * M_V_DIM:(hd + 1) * M_V_DIM].astype(BF16))
        qk_dot.append(_dot_nt(q_bf[hd], k_bf[hd]))
        qc_dot.append(_dot(q_bf[hd], c_ref[hd].astype(BF16)))
        qn_sum.append(jnp.sum(q * n_ref[hd], axis=-1, keepdims=True))

    for hd in heads:
        ig_col = gates[:, hd:hd + 1]
        ig_row = gates_t[hd:hd + 1, :]
        b_col = bcum[:, M_HEADS + hd:M_HEADS + hd + 1]
        b_row = bcum_t[M_HEADS + hd:M_HEADS + hd + 1, :]
        m_prev = m_ref[hd][:, 0:1]

        dmat = jnp.where(causal, b_col - b_row + ig_row, NEG_INF)
        inter = b_col + m_prev
        m_t = jnp.maximum(inter, jnp.max(dmat, axis=-1, keepdims=True))
        s = qk_dot[hd] * jnp.exp(dmat - m_t)
        e_inter = jnp.exp(inter - m_t)
        num = e_inter * qc_dot[hd] + _dot(s.astype(BF16), v_bf[hd])
        den = e_inter * qn_sum[hd] + jnp.sum(s, axis=-1, keepdims=True)
        h = num / jnp.maximum(jnp.abs(den), jnp.exp(-m_t))

        hn = h * lax.rsqrt(jnp.mean(h * h, axis=-1, keepdims=True) + EPS)
        hn = hn * gn_ref[:, hd * M_V_DIM:(hd + 1) * M_V_DIM]
        og = _sigmoid(o_ref[:, hd * M_V_DIM:(hd + 1) * M_V_DIM])
        out_ref[:, hd * M_V_DIM:(hd + 1) * M_V_DIM] = (hn * og).astype(out_ref.dtype)

    for hd in heads:
        ig_col = gates[:, hd:hd + 1]
        b_col = bcum[:, M_HEADS + hd:M_HEADS + hd + 1]
        m_prev = m_ref[hd][:, 0:1]
        g_last = b_col[lc - 1:lc, :]
        a_col = g_last - b_col + ig_col
        m_new = jnp.maximum(g_last + m_prev, jnp.max(a_col, axis=0, keepdims=True))
        w_col = jnp.exp(a_col - m_new)
        decay = jnp.exp(g_last + m_prev - m_new)
        kw = k_f32[hd] * w_col
        c_ref[hd] = decay * c_ref[hd] + _dot(kw.T.astype(BF16), v_bf[hd])
        n_ref[hd] = decay * n_ref[hd] + jnp.sum(kw, axis=0, keepdims=True)
        m_ref[hd] = jnp.broadcast_to(m_new, (1, LANES))


def _mlstm(zm, conv_w, conv_b, gate_b, out_norm, *, batch, seq, lc):
    t = zm.shape[0]
    nc = seq // lc
    qkw = 2 * M_QK_W
    return pl.pallas_call(
        functools.partial(_mlstm_kernel, lc=lc),
        out_shape=jax.ShapeDtypeStruct((t, M_V_W), BF16),
        grid=(batch, nc),
        in_specs=[
            pl.BlockSpec((lc, qkw), lambda b, c: (b * nc + c, 0)),
            pl.BlockSpec((lc, M_V_W), lambda b, c: (b * nc + c, qkw // M_V_W)),
            pl.BlockSpec((lc, M_V_W), lambda b, c: (b * nc + c, qkw // M_V_W + 1)),
            pl.BlockSpec((lc, GATE_W), lambda b, c: (b * nc + c, (qkw + 2 * M_V_W) // GATE_W)),
            pl.BlockSpec((M_CONV, qkw), lambda b, c: (0, 0)),
            pl.BlockSpec((1, qkw), lambda b, c: (0, 0)),
            pl.BlockSpec((1, GATE_W), lambda b, c: (0, 0)),
            pl.BlockSpec((1, M_V_W), lambda b, c: (0, 0)),
        ],
        out_specs=pl.BlockSpec((lc, M_V_W), lambda b, c: (b * nc + c, 0)),
        scratch_shapes=[
            pltpu.VMEM((2 * SUBLANES, qkw), F32),
            pltpu.VMEM((M_HEADS, M_QK_DIM, M_V_DIM), F32),
            pltpu.VMEM((M_HEADS, 1, M_QK_DIM), F32),
            pltpu.VMEM((M_HEADS, 1, LANES), F32),
        ],
        compiler_params=_cparams("arbitrary", "arbitrary"),
        name="mlstm",
    )(zm, zm, zm, zm, conv_w, conv_b, gate_b, out_norm)


MOBA_GROUP = 4
MOBA_HEADS = 4
MASK_BIG = 1e30
VT_ROWS = A_HEAD_DIM + BF16_SUBLANES


def _moba_kernel(q_ref, k_ref, v_ref, o_ref, kaug_ref, vt_ref, kmean_ref, ksplit_ref, qaug_ref,
                 m_ref, alpha_ref, acc_ref, s_ref, p_ref, *, nblk):
    qi = pl.program_id(2)
    blk = MOBA_BLOCK
    hd = A_HEAD_DIM
    grp = MOBA_GROUP
    nh = MOBA_HEADS
    heads = range(nh)
    c = (A_HEAD_DIM ** -0.5) * LOG2_E

    @pl.when(qi == 0)
    def _():
        lane = lax.broadcasted_iota(jnp.int32, (blk, hd), 1)
        row = lax.broadcasted_iota(jnp.int32, (VT_ROWS - hd, grp * blk), 0)
        ones_rows = (row == 0).astype(BF16)

        def prep(g, carry):
            for h in heads:
                vt_ref[h, g, hd:VT_ROWS, :] = ones_rows
            for gg in range(grp):
                j = g * grp + gg
                r0 = pl.multiple_of(j * blk, blk)
                for h in heads:
                    kj = k_ref[pl.ds(r0, blk), h * hd:(h + 1) * hd]
                    kaug_ref[h, pl.ds(r0, blk), 0:hd] = kj
                    kaug_ref[h, pl.ds(r0, blk), hd:2 * hd] = (lane == nblk * h + j).astype(BF16)
                    kmean_ref[h, pl.ds(j, 1), :] = jnp.mean(kj.astype(F32), axis=0, keepdims=True)
                    vt_ref[h, g, 0:hd, gg * blk:(gg + 1) * blk] = (
                        v_ref[pl.ds(r0, blk), h * hd:(h + 1) * hd].astype(F32).T.astype(BF16))
            return carry
        lax.fori_loop(0, nblk // grp, prep, 0)
        for h in heads:
            rest = kmean_ref[h]
            for part in range(3):
                term = rest.astype(BF16)
                ksplit_ref[h, part] = term
                rest = rest - term.astype(F32)

    blk_id = lax.broadcasted_iota(jnp.int32, (nblk, blk), 0)
    blk_f = blk_id.astype(F32)
    past = blk_id < qi

    def scores(h):
        q = q_ref[:, h * hd:(h + 1) * hd]
        return (_dot_nt(ksplit_ref[h, 0], q) + _dot_nt(ksplit_ref[h, 1], q)) + _dot_nt(ksplit_ref[h, 2], q)

    def select(h, sc):
        q = q_ref[:, h * hd:(h + 1) * hd]
        sc = jnp.where(past, sc, NEG_INF)
        bias = jnp.where(blk_id == qi, 0.0, -MASK_BIG)
        for _ in range(MOBA_TOPK):
            mx = jnp.max(sc, axis=0, keepdims=True)
            first = jnp.min(jnp.where(sc == mx, blk_f, float(nblk)), axis=0, keepdims=True)
            first = jnp.where(mx > NEG_INF, first, -1.0)
            pick = blk_f == first
            bias = jnp.where(pick, 0.0, bias)
            sc = jnp.where(pick, NEG_INF, sc)
        pieces = []
        if h > 0:
            pieces.append(jnp.zeros((h * nblk, blk), F32))
        pieces.append(bias)
        if (h + 1) * nblk < hd:
            pieces.append(jnp.zeros((hd - (h + 1) * nblk, blk), F32))
        qaug_ref[h, :, 0:hd] = q
        qaug_ref[h, :, hd:2 * hd] = jnp.concatenate(pieces, axis=0).T.astype(BF16)

    def logits(g, h):
        c0 = pl.multiple_of(g * (grp * blk), grp * blk)
        return _dot_nt(kaug_ref[h, pl.ds(c0, grp * blk), :], qaug_ref[h])

    def softmax(g, h, causal, keys=grp * blk):
        sg = s_ref[h, 0:keys, :]
        if causal:
            kpos = g * (grp * blk) + lax.broadcasted_iota(jnp.int32, sg.shape, 0)
            qpos = qi * blk + lax.broadcasted_iota(jnp.int32, sg.shape, 1)
            sg = jnp.where(kpos <= qpos, sg, -MASK_BIG)
        m_old = m_ref[h]
        m_new = jnp.maximum(m_old, jnp.max(sg, axis=0, keepdims=True))
        alpha_ref[h] = jnp.exp2((m_old - m_new) * c)
        p_ref[h, 0:keys, :] = jnp.exp2(((sg - m_new) * c).astype(BF16))
        m_ref[h] = m_new

    def accumulate(g, h, keys=grp * blk):
        acc_ref[h] = alpha_ref[h] * acc_ref[h] + _dot(vt_ref[h, g, :, 0:keys], p_ref[h, 0:keys, :])

    m_ref[...] = jnp.full(m_ref.shape, -MASK_BIG, F32)
    acc_ref[...] = jnp.zeros(acc_ref.shape, F32)

    last = qi // grp
    sc_all = [scores(h) for h in heads]
    for h in heads:
        select(h, sc_all[h])
        s_ref[h] = logits(0, h)

    def body(g, carry):
        nxt = [logits(g + 1, h) for h in heads]
        for h in heads:
            softmax(g, h, False)
            accumulate(g, h)
        for h in heads:
            s_ref[h] = nxt[h]
        return carry
    lax.fori_loop(0, last, body, 0)
    for own in range(grp):
        @pl.when(qi % grp == own)
        def _():
            for h in heads:
                softmax(last, h, True, keys=(own + 1) * blk)
                accumulate(last, h, keys=(own + 1) * blk)

    for h in heads:
        acc = acc_ref[h]
        o_ref[:, h * hd:(h + 1) * hd] = (acc[0:hd] / acc[hd:hd + 1]).T.astype(o_ref.dtype)


def _moba(qkv, *, batch, seq):
    t = qkv.shape[0]
    nblk = seq // MOBA_BLOCK
    hd = A_HEAD_DIM
    nh = MOBA_HEADS
    hw = nh * hd
    assert nblk % MOBA_GROUP == 0 and nh * nblk <= hd and A_HEADS % nh == 0
    ngrp = A_HEADS // nh
    return pl.pallas_call(
        functools.partial(_moba_kernel, nblk=nblk),
        out_shape=jax.ShapeDtypeStruct((t, A_W), BF16),
        grid=(batch, ngrp, nblk),
        in_specs=[
            pl.BlockSpec((MOBA_BLOCK, hw), lambda b, h, i: (b * nblk + i, h)),
            pl.BlockSpec((seq, hw), lambda b, h, i: (b, ngrp + h), pipeline_mode=pl.Buffered(1)),
            pl.BlockSpec((seq, hw), lambda b, h, i: (b, 2 * ngrp + h), pipeline_mode=pl.Buffered(1)),
        ],
        out_specs=pl.BlockSpec((MOBA_BLOCK, hw), lambda b, h, i: (b * nblk + i, h)),
        scratch_shapes=[
            pltpu.VMEM((nh, seq, 2 * hd), BF16),
            pltpu.VMEM((nh, nblk // MOBA_GROUP, VT_ROWS, MOBA_GROUP * MOBA_BLOCK), BF16),
            pltpu.VMEM((nh, nblk, hd), F32),
            pltpu.VMEM((nh, 3, nblk, hd), BF16),
            pltpu.VMEM((nh, MOBA_BLOCK, 2 * hd), BF16),
            pltpu.VMEM((nh, 1, MOBA_BLOCK), F32),
            pltpu.VMEM((nh, 1, MOBA_BLOCK), F32),
            pltpu.VMEM((nh, VT_ROWS, MOBA_BLOCK), F32),
            pltpu.VMEM((nh, MOBA_GROUP * MOBA_BLOCK, MOBA_BLOCK), F32),
            pltpu.VMEM((nh, MOBA_GROUP * MOBA_BLOCK, MOBA_BLOCK), BF16),
        ],
        compiler_params=_cparams("arbitrary", "arbitrary", "arbitrary"),
        name="moba",
    )(qkv, qkv, qkv)


def _merge_kernel(h_ref, hm_ref, ha_ref, wgm_ref, wga_ref, wm_ref, wa_ref, o_ref):
    h = h_ref[...]
    gm = _sigmoid(_dot(h, wgm_ref[...]))
    ga = _sigmoid(_dot(h, wga_ref[...]))
    um = _dot(hm_ref[...], wm_ref[...])
    ua = _dot(ha_ref[...], wa_ref[...])
    o_ref[...] = (gm * um + ga * ua).astype(o_ref.dtype)


def _merge(h, hm, ha, wgm, wga, wm, wa, *, tm, tn):
    t, d = h.shape
    tm = min(tm, t)
    return pl.pallas_call(
        _merge_kernel,
        out_shape=jax.ShapeDtypeStruct((t, d), BF16),
        grid=(t // tm, d // tn),
        in_specs=[
            pl.BlockSpec((tm, d), lambda i, j: (i, 0)),
            pl.BlockSpec((tm, M_V_W), lambda i, j: (i, 0)),
            pl.BlockSpec((tm, A_W), lambda i, j: (i, 0)),
            pl.BlockSpec((d, tn), lambda i, j: (0, j)),
            pl.BlockSpec((d, tn), lambda i, j: (0, j)),
            pl.BlockSpec((M_V_W, tn), lambda i, j: (0, j)),
            pl.BlockSpec((A_W, tn), lambda i, j: (0, j)),
        ],
        out_specs=pl.BlockSpec((tm, tn), lambda i, j: (i, j)),
        compiler_params=_cparams("arbitrary", "arbitrary"),
        name="merge",
    )(h, hm, ha, wgm, wga, wm, wa)


def _outproj_kernel(x_ref, a_ref, w_ref, g_ref, o_ref, h_ref):
    y = x_ref[...] + _dot(a_ref[...], w_ref[...])
    o_ref[...] = y
    h_ref[...] = _rmsnorm(y, g_ref[...]).astype(BF16)


def _outproj(x, a, w, g, *, tm):
    t, d = x.shape
    k = a.shape[1]
    tm = min(tm, t)
    return pl.pallas_call(
        _outproj_kernel,
        out_shape=(jax.ShapeDtypeStruct((t, d), F32), jax.ShapeDtypeStruct((t, d), BF16)),
        grid=(t // tm,),
        in_specs=[
            pl.BlockSpec((tm, d), lambda i: (i, 0)),
            pl.BlockSpec((tm, k), lambda i: (i, 0)),
            _resident((k, d)),
            _resident((1, d)),
        ],
        out_specs=(pl.BlockSpec((tm, d), lambda i: (i, 0)),
                   pl.BlockSpec((tm, d), lambda i: (i, 0))),
        compiler_params=_cparams("arbitrary"),
        name="outproj",
    )(x, a, w, g)


def _ffn_kernel(h_ref, w1_ref, w2_ref, o_ref):
    def mlp():
        u = jnp.maximum(_dot(h_ref[...], w1_ref[...]), 0.0)
        return _dot((u * u).astype(BF16), w2_ref[...])

    @pl.when(pl.program_id(1) == 0)
    def _():
        o_ref[...] = mlp()

    @pl.when(pl.program_id(1) != 0)
    def _():
        o_ref[...] += mlp()


def _ffn(h, w1, w2, *, tm, tf):
    t, d = h.shape
    dff = w1.shape[1]
    tm = min(tm, t)
    return pl.pallas_call(
        _ffn_kernel,
        out_shape=jax.ShapeDtypeStruct((t, d), F32),
        grid=(t // tm, dff // tf),
        in_specs=[
            pl.BlockSpec((tm, d), lambda i, f: (i, 0)),
            pl.BlockSpec((d, tf), lambda i, f: (0, f)),
            pl.BlockSpec((tf, d), lambda i, f: (f, 0)),
        ],
        out_specs=pl.BlockSpec((tm, d), lambda i, f: (i, 0)),
        compiler_params=_cparams("arbitrary", "arbitrary"),
        name="ffn",
    )(h, w1, w2)


def _ple_kernel(x_ref, f_ref, p_ref, g_ref, wg_ref, wp_ref, gf_ref, o_ref, *, final):
    x = x_ref[...] + f_ref[...]
    h = _rmsnorm(x, g_ref[...]).astype(BF16)
    gate = _sigmoid(_dot(h, wg_ref[...]))
    emb = _dot(p_ref[...].astype(BF16), wp_ref[...])
    y = x + gate * emb
    o_ref[...] = _rmsnorm(y, gf_ref[...]) if final else y


def _ple(x, f, p, g, wg, wp, gf, *, tm, final):
    t, d = x.shape
    tm = min(tm, t)
    return pl.pallas_call(
        functools.partial(_ple_kernel, final=final),
        out_shape=jax.ShapeDtypeStruct((t, d), F32),
        grid=(t // tm,),
        in_specs=[
            pl.BlockSpec((tm, d), lambda i: (i, 0)),
            pl.BlockSpec((tm, d), lambda i: (i, 0)),
            pl.BlockSpec((tm, PLE_DIM), lambda i: (i, 0)),
            _resident((1, d)),
            _resident((d, d)),
            _resident((PLE_DIM, d)),
            _resident((1, d)),
        ],
        out_specs=pl.BlockSpec((tm, d), lambda i: (i, 0)),
        compiler_params=_cparams("arbitrary"),
        name="ple",
    )(x, f, p, g, wg, wp, gf)


def _layer(x2d, p2d, pos2d, batch, seq, final_norm, final, attn_norm, w_in, b_if, conv_w, conv_b,
           m_out_norm, w_up_m, w_up_a, w_out, mlp_norm, w_ff1, w_ff2, ple_norm, w_ple_gate,
           w_ple_proj):
    d = D_MODEL
    o_if = 2 * M_QK_W + 2 * M_V_W
    o_a = o_if + 2 * M_HEADS
    o_g = o_a + 3 * A_W
    w_gate = jnp.pad(w_in[:, o_if:o_a], ((0, 0), (0, GATE_W - 2 * M_HEADS))).astype(BF16)
    w_m = w_in[:, :o_if].astype(BF16)
    w_a = w_in[:, o_a:o_g].astype(BF16)
    w_gm = w_in[:, o_g:o_g + d].astype(BF16)
    w_ga = w_in[:, o_g + d:].astype(BF16)
    gate_b = jnp.pad(b_if.reshape(1, 2 * M_HEADS), ((0, 0), (0, GATE_W - 2 * M_HEADS)))
    g_attn = attn_norm.reshape(1, d)

    half = ROPE_DIM // 2
    inv_freq = ROPE_THETA ** (-jnp.arange(half, dtype=F32) * 2.0 / ROPE_DIM)
    freq = jnp.concatenate([-inv_freq, inv_freq, jnp.zeros((A_HEAD_DIM - ROPE_DIM,), F32)]).reshape(1, -1)

    zm, h = _inproj_m(x2d, g_attn, w_m, w_gate, tm=TM_RESIDENT)
    qkv = _inproj_a(h, w_a, pos2d, freq, tm=TM_RESIDENT_BF16)
    hm = _mlstm(zm, conv_w, conv_b.reshape(1, -1), gate_b, m_out_norm.reshape(1, -1),
                batch=batch, seq=seq, lc=MLSTM_CHUNK)
    ha = _moba(qkv, batch=batch, seq=seq)
    merged = _merge(h, hm, ha, w_gm, w_ga, w_up_m.astype(BF16), w_up_a.astype(BF16),
                    tm=TM_STREAMED, tn=MERGE_TN)
    x1, h1 = _outproj(x2d, merged, w_out.astype(BF16), mlp_norm.reshape(1, d), tm=TM_RESIDENT)
    mlp = _ffn(h1, w_ff1.astype(BF16), w_ff2.astype(BF16), tm=TM_STREAMED, tf=FFN_TF)
    return _ple(x1, mlp, p2d, ple_norm.reshape(1, d), w_ple_gate.astype(BF16), w_ple_proj.astype(BF16),
                final_norm.reshape(1, d), tm=TM_RESIDENT, final=final)


def kernel(x, p, positions, attn_norm, w_in, b_if, conv_w, conv_b, m_out_norm, w_up_m, w_up_a,
           w_out, mlp_norm, w_ff1, w_ff2, ple_norm, w_ple_gate, w_ple_proj, final_norm):
    batch, seq, d = x.shape
    depth = w_in.shape[0]
    t = batch * seq
    x2d = x.reshape(t, d)
    pos2d = positions.reshape(t, 1)
    for i in range(depth):
        x2d = _layer(
            x2d, p[i].reshape(t, PLE_DIM), pos2d, batch, seq, final_norm, i == depth - 1,
            attn_norm[i], w_in[i], b_if[i], conv_w[i], conv_b[i], m_out_norm[i], w_up_m[i],
            w_up_a[i], w_out[i], mlp_norm[i], w_ff1[i], w_ff2[i], ple_norm[i], w_ple_gate[i],
            w_ple_proj[i])
    return x2d.reshape(batch, seq, d)
```

```python
import functools

import jax
import jax.numpy as jnp
from jax import lax
from jax.experimental import pallas as pl
from jax.experimental.pallas import tpu as pltpu

D_MODEL = 2048
PLE_DIM = 256
M_HEADS = 4
M_QK_DIM = 128
M_V_DIM = 256
M_CONV = 4
A_HEADS = 8
A_HEAD_DIM = 128
MOBA_BLOCK = 256
MOBA_TOPK = 3
ROPE_THETA = 500000.0
ROPE_DIM = A_HEAD_DIM // 4
D_FF = 4 * D_MODEL
EPS = 1e-6

M_QK_W = M_HEADS * M_QK_DIM
M_V_W = M_HEADS * M_V_DIM
A_W = A_HEADS * A_HEAD_DIM

LANES = 128
SUBLANES = 8
BF16_SUBLANES = 16
GATE_W = LANES
ZM_W = 2 * M_QK_W + 2 * M_V_W + GATE_W
VMEM_LIMIT = 56 * 1024 * 1024

TM_RESIDENT = 512
TM_RESIDENT_BF16 = 1024
TM_STREAMED = 1024
MERGE_TN = 512
FFN_TF = 1024
MLSTM_CHUNK = 256

F32 = jnp.float32
BF16 = jnp.bfloat16
NEG_INF = float("-inf")
LOG2_E = 1.4426950408889634


def _cparams(*sem):
    return pltpu.CompilerParams(dimension_semantics=sem, vmem_limit_bytes=VMEM_LIMIT)


def _rmsnorm(x, g):
    ms = jnp.mean(x * x, axis=-1, keepdims=True)
    return (x * lax.rsqrt(ms + EPS)) * g


def _sigmoid(x):
    return 1.0 / (1.0 + jnp.exp(-x))


def _dot(a, b):
    return jnp.dot(a, b, preferred_element_type=F32)


def _dot_nt(a, b):
    return lax.dot_general(a, b, (((1,), (1,)), ((), ())), preferred_element_type=F32)


def _resident(shape):
    return pl.BlockSpec(shape, lambda *_: (0,) * len(shape), pipeline_mode=pl.Buffered(1))


def _inproj_m_kernel(x_ref, g_ref, w_ref, wg_ref, o_ref, h_ref):
    h = _rmsnorm(x_ref[...], g_ref[...]).astype(BF16)
    h_ref[...] = h
    n = w_ref.shape[1]
    o_ref[:, 0:n] = _dot(h, w_ref[...])
    o_ref[:, n:] = _dot(h, wg_ref[...])


def _inproj_m(x, g, w, w_gate, *, tm):
    t, d = x.shape
    n = w.shape[1]
    ng = w_gate.shape[1]
    tm = min(tm, t)
    return pl.pallas_call(
        _inproj_m_kernel,
        out_shape=(jax.ShapeDtypeStruct((t, n + ng), F32), jax.ShapeDtypeStruct((t, d), BF16)),
        grid=(t // tm,),
        in_specs=[
            pl.BlockSpec((tm, d), lambda i: (i, 0)),
            _resident((1, d)),
            _resident((d, n)),
            _resident((d, ng)),
        ],
        out_specs=(pl.BlockSpec((tm, n + ng), lambda i: (i, 0)),
                   pl.BlockSpec((tm, d), lambda i: (i, 0))),
        compiler_params=_cparams("arbitrary"),
        name="inproj_m",
    )(x, g, w, w_gate)


def _inproj_a_kernel(h_ref, w_ref, pos_ref, freq_ref, o_ref):
    h = h_ref[...]
    ang = pos_ref[...].astype(F32) * freq_ref[...]
    c = jnp.cos(ang)
    s = jnp.sin(ang)
    lane = lax.broadcasted_iota(jnp.int32, c.shape, 1)
    half = ROPE_DIM // 2
    for part in range(2):
        z = _dot(h, w_ref[:, part * A_W:(part + 1) * A_W])
        for hd in range(A_HEADS):
            zh = z[:, hd * A_HEAD_DIM:(hd + 1) * A_HEAD_DIM]
            partner = jnp.where(lane < half,
                                pltpu.roll(zh, A_HEAD_DIM - half, 1),
                                pltpu.roll(zh, half, 1))
            rot = jnp.where(lane < ROPE_DIM, zh * c + partner * s, zh)
            c0 = part * A_W + hd * A_HEAD_DIM
            o_ref[:, c0:c0 + A_HEAD_DIM] = rot.astype(o_ref.dtype)
    o_ref[:, 2 * A_W:] = _dot(h, w_ref[:, 2 * A_W:]).astype(o_ref.dtype)


def _inproj_a(h, w, pos, freq, *, tm):
    t, d = h.shape
    n = w.shape[1]
    tm = min(tm, t)
    return pl.pallas_call(
        _inproj_a_kernel,
        out_shape=jax.ShapeDtypeStruct((t, n), BF16),
        grid=(t // tm,),
        in_specs=[
            pl.BlockSpec((tm, d), lambda i: (i, 0)),
            _resident((d, n)),
            pl.BlockSpec((tm, 1), lambda i: (i, 0)),
            _resident((1, A_HEAD_DIM)),
        ],
        out_specs=pl.BlockSpec((tm, n), lambda i: (i, 0)),
        compiler_params=_cparams("arbitrary"),
        name="inproj_a",
    )(h, w, pos, freq)


def _mlstm_kernel(qk_ref, v_ref, o_ref, gate_ref, cw_ref, cb_ref, gb_ref, gn_ref, out_ref,
                  buf_ref, c_ref, n_ref, m_ref, *, lc):
    ci = pl.program_id(1)
    pad = SUBLANES

    @pl.when(ci == 0)
    def _():
        buf_ref[0:pad, :] = jnp.zeros((pad, 2 * M_QK_W), F32)
        c_ref[...] = jnp.zeros(c_ref.shape, F32)
        n_ref[...] = jnp.zeros(n_ref.shape, F32)
        m_ref[...] = jnp.zeros(m_ref.shape, F32)

    def conv(window):
        y = cb_ref[...] + cw_ref[M_CONV - 1:M_CONV, :] * window(0)
        for j in range(1, M_CONV):
            y = y + cw_ref[M_CONV - 1 - j:M_CONV - j, :] * window(j)
        return y

    buf_ref[pad:2 * pad, :] = qk_ref[0:pad, :]
    y_head = conv(lambda j: buf_ref[pad - j:2 * pad - j, :])
    y_body = conv(lambda j: qk_ref[pad - j:lc - j, :])
    buf_ref[0:pad, :] = qk_ref[lc - pad:lc, :]
    y = jnp.concatenate([y_head, y_body], axis=0)
    qk = y * _sigmoid(y)

    gates = gate_ref[...] + gb_ref[...]
    lf = jnp.minimum(gates, 0.0) - jnp.log1p(jnp.exp(-jnp.abs(gates)))
    row = lax.broadcasted_iota(jnp.int32, (lc, lc), 0)
    col = lax.broadcasted_iota(jnp.int32, (lc, lc), 1)
    causal = row >= col
    tri = causal.astype(F32)
    bcum = jnp.dot(tri, lf, preferred_element_type=F32, precision=lax.Precision.HIGHEST)
    gates_t = gates.T
    bcum_t = bcum.T

    heads = range(M_HEADS)
    q_bf, k_f32, k_bf, v_bf, qk_dot, qc_dot, qn_sum = [], [], [], [], [], [], []
    for hd in heads:
        q = qk[:, hd * M_QK_DIM:(hd + 1) * M_QK_DIM]
        k = qk[:, M_QK_W + hd * M_QK_DIM:M_QK_W + (hd + 1) * M_QK_DIM] * (M_QK_DIM ** -0.5)
        q_bf.append(q.astype(BF16))
        k_f32.append(k)
        k_bf.append(k.astype(BF16))
        v_bf.append(v_ref[:, hd * M_V_DIM:(hd + 1) * M_V_DIM].astype(BF16))
        qk_dot.append(_dot_nt(q_bf[hd], k_bf[hd]))
        qc_dot.append(_dot(q_bf[hd], c_ref[hd].astype(BF16)))
        qn_sum.append(jnp.sum(q * n_ref[hd], axis=-1, keepdims=True))

    for hd in heads:
        ig_col = gates[:, hd:hd + 1]
        ig_row = gates_t[hd:hd + 1, :]
        b_col = bcum[:, M_HEADS + hd:M_HEADS + hd + 1]
        b_row = bcum_t[M_HEADS + hd:M_HEADS + hd + 1, :]
        m_prev = m_ref[hd][:, 0:1]

        dmat = jnp.where(causal, b_col - b_row + ig_row, NEG_INF)
        inter = b_col + m_prev
        m_t = jnp.maximum(inter, jnp.max(dmat, axis=-1, keepdims=True))
        s = qk_dot[hd] * jnp.exp(dmat - m_t)
        e_inter = jnp.exp(inter - m_t)
        num = e_inter * qc_dot[hd] + _dot(s.astype(BF16), v_bf[hd])
        den = e_inter * qn_sum[hd] + jnp.sum(s, axis=-1, keepdims=True)
        h = num / jnp.maximum(jnp.abs(den), jnp.exp(-m_t))

        hn = h * lax.rsqrt(jnp.mean(h * h, axis=-1, keepdims=True) + EPS)
        hn = hn * gn_ref[:, hd * M_V_DIM:(hd + 1) * M_V_DIM]
        og = _sigmoid(o_ref[:, hd * M_V_DIM:(hd + 1) * M_V_DIM])
        out_ref[:, hd * M_V_DIM:(hd + 1) * M_V_DIM] = (hn * og).astype(out_ref.dtype)

    for hd in heads:
        ig_col = gates[:, hd:hd + 1]
        b_col = bcum[:, M_HEADS + hd:M_HEADS + hd + 1]
        m_prev = m_ref[hd][:, 0:1]
        g_last = b_col[lc - 1:lc, :]
        a_col = g_last - b_col + ig_col
        m_new = jnp.maximum(g_last + m_prev, jnp.max(a_col, axis=0, keepdims=True))
        w_col = jnp.exp(a_col - m_new)
        decay = jnp.exp(g_last + m_prev - m_new)
        kw = k_f32[hd] * w_col
        c_ref[hd] = decay * c_ref[hd] + _dot(kw.T.astype(BF16), v_bf[hd])
        n_ref[hd] = decay * n_ref[hd] + jnp.sum(kw, axis=0, keepdims=True)
        m_ref[hd] = jnp.broadcast_to(m_new, (1, LANES))


def _mlstm(zm, conv_w, conv_b, gate_b, out_norm, *, batch, seq, lc):
    t = zm.shape[0]
    nc = seq // lc
    qkw = 2 * M_QK_W
    return pl.pallas_call(
        functools.partial(_mlstm_kernel, lc=lc),
        out_shape=jax.ShapeDtypeStruct((t, M_V_W), BF16),
        grid=(batch, nc),
        in_specs=[
            pl.BlockSpec((lc, qkw), lambda b, c: (b * nc + c, 0)),
            pl.BlockSpec((lc, M_V_W), lambda b, c: (b * nc + c, qkw // M_V_W)),
            pl.BlockSpec((lc, M_V_W), lambda b, c: (b * nc + c, qkw // M_V_W + 1)),
            pl.BlockSpec((lc, GATE_W), lambda b, c: (b * nc + c, (qkw + 2 * M_V_W) // GATE_W)),
            pl.BlockSpec((M_CONV, qkw), lambda b, c: (0, 0)),
            pl.BlockSpec((1, qkw), lambda b, c: (0, 0)),
            pl.BlockSpec((1, GATE_W), lambda b, c: (0, 0)),
            pl.BlockSpec((1, M_V_W), lambda b, c: (0, 0)),
        ],
        out_specs=pl.BlockSpec((lc, M_V_W), lambda b, c: (b * nc + c, 0)),
        scratch_shapes=[
            pltpu.VMEM((2 * SUBLANES, qkw), F32),
            pltpu.VMEM((M_HEADS, M_QK_DIM, M_V_DIM), F32),
            pltpu.VMEM((M_HEADS, 1, M_QK_DIM), F32),
            pltpu.VMEM((M_HEADS, 1, LANES), F32),
        ],
        compiler_params=_cparams("arbitrary", "arbitrary"),
        name="mlstm",
    )(zm, zm, zm, zm, conv_w, conv_b, gate_b, out_norm)


MOBA_GROUP = 4
MOBA_HEADS = 4
MASK_BIG = 1e30
VT_ROWS = A_HEAD_DIM + BF16_SUBLANES


def _moba_kernel(q_ref, k_ref, v_ref, o_ref, kaug_ref, vt_ref, kmean_ref, ksplit_ref, qaug_ref,
                 m_ref, mx_ref, alpha_ref, acc_ref, s_ref, p_ref, *, nblk):
    qi = pl.program_id(2)
    blk = MOBA_BLOCK
    hd = A_HEAD_DIM
    grp = MOBA_GROUP
    nh = MOBA_HEADS
    heads = range(nh)
    c = (A_HEAD_DIM ** -0.5) * LOG2_E

    @pl.when(qi == 0)
    def _():
        lane = lax.broadcasted_iota(jnp.int32, (blk, hd), 1)
        row = lax.broadcasted_iota(jnp.int32, (VT_ROWS - hd, grp * blk), 0)
        ones_rows = (row == 0).astype(BF16)

        def prep(g, carry):
            for h in heads:
                vt_ref[h, g, hd:VT_ROWS, :] = ones_rows
            for gg in range(grp):
                j = g * grp + gg
                r0 = pl.multiple_of(j * blk, blk)
                for h in heads:
                    kj = k_ref[pl.ds(r0, blk), h * hd:(h + 1) * hd]
                    kaug_ref[h, pl.ds(r0, blk), 0:hd] = kj
                    kaug_ref[h, pl.ds(r0, blk), hd:2 * hd] = (lane == nblk * h + j).astype(BF16)
                    kmean_ref[h, pl.ds(j, 1), :] = jnp.mean(kj.astype(F32), axis=0, keepdims=True)
                    vt_ref[h, g, 0:hd, gg * blk:(gg + 1) * blk] = (
                        v_ref[pl.ds(r0, blk), h * hd:(h + 1) * hd].astype(F32).T.astype(BF16))
            return carry
        lax.fori_loop(0, nblk // grp, prep, 0)
        for h in heads:
            rest = kmean_ref[h]
            for part in range(3):
                term = rest.astype(BF16)
                ksplit_ref[h, part] = term
                rest = rest - term.astype(F32)

    blk_id = lax.broadcasted_iota(jnp.int32, (nblk, blk), 0)
    blk_f = blk_id.astype(F32)
    past = blk_id < qi

    def scores(h):
        q = q_ref[:, h * hd:(h + 1) * hd]
        return (_dot_nt(ksplit_ref[h, 0], q) + _dot_nt(ksplit_ref[h, 1], q)) + _dot_nt(ksplit_ref[h, 2], q)

    def select(h, sc):
        q = q_ref[:, h * hd:(h + 1) * hd]
        sc = jnp.where(past, sc, NEG_INF)
        bias = jnp.where(blk_id == qi, 0.0, -MASK_BIG)
        for _ in range(MOBA_TOPK):
            mx = jnp.max(sc, axis=0, keepdims=True)
            first = jnp.min(jnp.where(sc == mx, blk_f, float(nblk)), axis=0, keepdims=True)
            first = jnp.where(mx > NEG_INF, first, -1.0)
            pick = blk_f == first
            bias = jnp.where(pick, 0.0, bias)
            sc = jnp.where(pick, NEG_INF, sc)
        pieces = []
        if h > 0:
            pieces.append(jnp.zeros((h * nblk, blk), F32))
        pieces.append(bias)
        if (h + 1) * nblk < hd:
            pieces.append(jnp.zeros((hd - (h + 1) * nblk, blk), F32))
        qaug_ref[h, :, 0:hd] = q
        qaug_ref[h, :, hd:2 * hd] = jnp.concatenate(pieces, axis=0).T.astype(BF16)

    def logits(g, h):
        c0 = pl.multiple_of(g * (grp * blk), grp * blk)
        return _dot_nt(kaug_ref[h, pl.ds(c0, grp * blk), :], qaug_ref[h])

    def softmax(h, own=None):
        m_old = m_ref[h]
        full = grp if own is None else own
        gmax = None
        for gg in range(full):
            gmax = mx_ref[h, gg] if gmax is None else jnp.maximum(gmax, mx_ref[h, gg])
        if own is not None:
            s_own = s_ref[h, own * blk:(own + 1) * blk, :]
            kpos = lax.broadcasted_iota(jnp.int32, s_own.shape, 0)
            qpos = lax.broadcasted_iota(jnp.int32, s_own.shape, 1)
            s_own = jnp.where(kpos <= qpos, s_own, -MASK_BIG)
            own_max = jnp.max(s_own, axis=0, keepdims=True)
            gmax = own_max if gmax is None else jnp.maximum(gmax, own_max)
        m_new = jnp.maximum(m_old, gmax)
        alpha_ref[h] = jnp.exp2((m_old - m_new) * c)
        if full:
            p_ref[h, 0:full * blk, :] = jnp.exp2(((s_ref[h, 0:full * blk, :] - m_new) * c).astype(BF16))
        if own is not None:
            p_ref[h, own * blk:(own + 1) * blk, :] = jnp.exp2(((s_own - m_new) * c).astype(BF16))
        m_ref[h] = m_new

    def accumulate(g, h, keys=grp * blk):
        acc_ref[h] = alpha_ref[h] * acc_ref[h] + _dot(vt_ref[h, g, :, 0:keys], p_ref[h, 0:keys, :])

    m_ref[...] = jnp.full(m_ref.shape, -MASK_BIG, F32)
    acc_ref[...] = jnp.zeros(acc_ref.shape, F32)

    last = qi // grp
    sc_all = [scores(h) for h in heads]
    def stash(h, sg):
        s_ref[h] = sg
        for gg in range(grp):
            mx_ref[h, gg] = jnp.max(sg[gg * blk:(gg + 1) * blk, :], axis=0, keepdims=True)

    for h in heads:
        select(h, sc_all[h])
        stash(h, logits(0, h))

    def body(g, carry):
        nxt = [logits(g + 1, h) for h in heads]
        for h in heads:
            softmax(h)
            accumulate(g, h)
        for h in heads:
            stash(h, nxt[h])
        return carry
    lax.fori_loop(0, last, body, 0)
    for own in range(grp):
        @pl.when(qi % grp == own)
        def _():
            for h in heads:
                softmax(h, own=own)
                accumulate(last, h, keys=(own + 1) * blk)

    for h in heads:
        acc = acc_ref[h]
        o_ref[:, h * hd:(h + 1) * hd] = (acc[0:hd] / acc[hd:hd + 1]).T.astype(o_ref.dtype)


def _moba(qkv, *, batch, seq):
    t = qkv.shape[0]
    nblk = seq // MOBA_BLOCK
    hd = A_HEAD_DIM
    nh = MOBA_HEADS
    hw = nh * hd
    assert nblk % MOBA_GROUP == 0 and nh * nblk <= hd and A_HEADS % nh == 0
    ngrp = A_HEADS // nh
    return pl.pallas_call(
        functools.partial(_moba_kernel, nblk=nblk),
        out_shape=jax.ShapeDtypeStruct((t, A_W), BF16),
        grid=(batch, ngrp, nblk),
        in_specs=[
            pl.BlockSpec((MOBA_BLOCK, hw), lambda b, h, i: (b * nblk + i, h)),
            pl.BlockSpec((seq, hw), lambda b, h, i: (b, ngrp + h), pipeline_mode=pl.Buffered(1)),
            pl.BlockSpec((seq, hw), lambda b, h, i: (b, 2 * ngrp + h), pipeline_mode=pl.Buffered(1)),
        ],
        out_specs=pl.BlockSpec((MOBA_BLOCK, hw), lambda b, h, i: (b * nblk + i, h)),
        scratch_shapes=[
            pltpu.VMEM((nh, seq, 2 * hd), BF16),
            pltpu.VMEM((nh, nblk // MOBA_GROUP, VT_ROWS, MOBA_GROUP * MOBA_BLOCK), BF16),
            pltpu.VMEM((nh, nblk, hd), F32),
            pltpu.VMEM((nh, 3, nblk, hd), BF16),
            pltpu.VMEM((nh, MOBA_BLOCK, 2 * hd), BF16),
            pltpu.VMEM((nh, 1, MOBA_BLOCK), F32),
            pltpu.VMEM((nh, MOBA_GROUP, 1, MOBA_BLOCK), F32),
            pltpu.VMEM((nh, 1, MOBA_BLOCK), F32),
            pltpu.VMEM((nh, VT_ROWS, MOBA_BLOCK), F32),
            pltpu.VMEM((nh, MOBA_GROUP * MOBA_BLOCK, MOBA_BLOCK), F32),
            pltpu.VMEM((nh, MOBA_GROUP * MOBA_BLOCK, MOBA_BLOCK), BF16),
        ],
        compiler_params=_cparams("arbitrary", "arbitrary", "arbitrary"),
        name="moba",
    )(qkv, qkv, qkv)


def _merge_kernel(h_ref, hm_ref, ha_ref, wgm_ref, wga_ref, wm_ref, wa_ref, o_ref):
    h = h_ref[...]
    gm = _sigmoid(_dot(h, wgm_ref[...]))
    ga = _sigmoid(_dot(h, wga_ref[...]))
    um = _dot(hm_ref[...], wm_ref[...])
    ua = _dot(ha_ref[...], wa_ref[...])
    o_ref[...] = (gm * um + ga * ua).astype(o_ref.dtype)


def _merge(h, hm, ha, wgm, wga, wm, wa, *, tm, tn):
    t, d = h.shape
    tm = min(tm, t)
    return pl.pallas_call(
        _merge_kernel,
        out_shape=jax.ShapeDtypeStruct((t, d), BF16),
        grid=(t // tm, d // tn),
        in_specs=[
            pl.BlockSpec((tm, d), lambda i, j: (i, 0)),
            pl.BlockSpec((tm, M_V_W), lambda i, j: (i, 0)),
            pl.BlockSpec((tm, A_W), lambda i, j: (i, 0)),
            pl.BlockSpec((d, tn), lambda i, j: (0, j)),
            pl.BlockSpec((d, tn), lambda i, j: (0, j)),
            pl.BlockSpec((M_V_W, tn), lambda i, j: (0, j)),
            pl.BlockSpec((A_W, tn), lambda i, j: (0, j)),
        ],
        out_specs=pl.BlockSpec((tm, tn), lambda i, j: (i, j)),
        compiler_params=_cparams("arbitrary", "arbitrary"),
        name="merge",
    )(h, hm, ha, wgm, wga, wm, wa)


def _outproj_kernel(x_ref, a_ref, w_ref, g_ref, o_ref, h_ref):
    y = x_ref[...] + _dot(a_ref[...], w_ref[...])
    o_ref[...] = y
    h_ref[...] = _rmsnorm(y, g_ref[...]).astype(BF16)


def _outproj(x, a, w, g, *, tm):
    t, d = x.shape
    k = a.shape[1]
    tm = min(tm, t)
    return pl.pallas_call(
        _outproj_kernel,
        out_shape=(jax.ShapeDtypeStruct((t, d), F32), jax.ShapeDtypeStruct((t, d), BF16)),
        grid=(t // tm,),
        in_specs=[
            pl.BlockSpec((tm, d), lambda i: (i, 0)),
            pl.BlockSpec((tm, k), lambda i: (i, 0)),
            _resident((k, d)),
            _resident((1, d)),
        ],
        out_specs=(pl.BlockSpec((tm, d), lambda i: (i, 0)),
                   pl.BlockSpec((tm, d), lambda i: (i, 0))),
        compiler_params=_cparams("arbitrary"),
        name="outproj",
    )(x, a, w, g)


def _ffn_kernel(h_ref, w1_ref, w2_ref, o_ref):
    def mlp():
        u = jnp.maximum(_dot(h_ref[...], w1_ref[...]), 0.0)
        return _dot((u * u).astype(BF16), w2_ref[...])

    @pl.when(pl.program_id(1) == 0)
    def _():
        o_ref[...] = mlp()

    @pl.when(pl.program_id(1) != 0)
    def _():
        o_ref[...] += mlp()


def _ffn(h, w1, w2, *, tm, tf):
    t, d = h.shape
    dff = w1.shape[1]
    tm = min(tm, t)
    return pl.pallas_call(
        _ffn_kernel,
        out_shape=jax.ShapeDtypeStruct((t, d), F32),
        grid=(t // tm, dff // tf),
        in_specs=[
            pl.BlockSpec((tm, d), lambda i, f: (i, 0)),
            pl.BlockSpec((d, tf), lambda i, f: (0, f)),
            pl.BlockSpec((tf, d), lambda i, f: (f, 0)),
        ],
        out_specs=pl.BlockSpec((tm, d), lambda i, f: (i, 0)),
        compiler_params=_cparams("arbitrary", "arbitrary"),
        name="ffn",
    )(h, w1, w2)


def _ple_kernel(x_ref, f_ref, p_ref, g_ref, wg_ref, wp_ref, gf_ref, o_ref, *, final):
    x = x_ref[...] + f_ref[...]
    h = _rmsnorm(x, g_ref[...]).astype(BF16)
    gate = _sigmoid(_dot(h, wg_ref[...]))
    emb = _dot(p_ref[...].astype(BF16), wp_ref[...])
    y = x + gate * emb
    o_ref[...] = _rmsnorm(y, gf_ref[...]) if final else y


def _ple(x, f, p, g, wg, wp, gf, *, tm, final):
    t, d = x.shape
    tm = min(tm, t)
    return pl.pallas_call(
        functools.partial(_ple_kernel, final=final),
        out_shape=jax.ShapeDtypeStruct((t, d), F32),
        grid=(t // tm,),
        in_specs=[
            pl.BlockSpec((tm, d), lambda i: (i, 0)),
            pl.BlockSpec((tm, d), lambda i: (i, 0)),
            pl.BlockSpec((tm, PLE_DIM), lambda i: (i, 0)),
            _resident((1, d)),
            _resident((d, d)),
            _resident((PLE_DIM, d)),
            _resident((1, d)),
        ],
        out_specs=pl.BlockSpec((tm, d), lambda i: (i, 0)),
        compiler_params=_cparams("arbitrary"),
        name="ple",
    )(x, f, p, g, wg, wp, gf)


def _layer(x2d, p2d, pos2d, batch, seq, final_norm, final, attn_norm, w_in, b_if, conv_w, conv_b,
           m_out_norm, w_up_m, w_up_a, w_out, mlp_norm, w_ff1, w_ff2, ple_norm, w_ple_gate,
           w_ple_proj):
    d = D_MODEL
    o_if = 2 * M_QK_W + 2 * M_V_W
    o_a = o_if + 2 * M_HEADS
    o_g = o_a + 3 * A_W
    w_gate = jnp.pad(w_in[:, o_if:o_a], ((0, 0), (0, GATE_W - 2 * M_HEADS))).astype(BF16)
    w_m = w_in[:, :o_if].astype(BF16)
    w_a = w_in[:, o_a:o_g].astype(BF16)
    w_gm = w_in[:, o_g:o_g + d].astype(BF16)
    w_ga = w_in[:, o_g + d:].astype(BF16)
    gate_b = jnp.pad(b_if.reshape(1, 2 * M_HEADS), ((0, 0), (0, GATE_W - 2 * M_HEADS)))
    g_attn = attn_norm.reshape(1, d)

    half = ROPE_DIM // 2
    inv_freq = ROPE_THETA ** (-jnp.arange(half, dtype=F32) * 2.0 / ROPE_DIM)
    freq = jnp.concatenate([-inv_freq, inv_freq, jnp.zeros((A_HEAD_DIM - ROPE_DIM,), F32)]).reshape(1, -1)

    zm, h = _inproj_m(x2d, g_attn, w_m, w_gate, tm=TM_RESIDENT)
    qkv = _inproj_a(h, w_a, pos2d, freq, tm=TM_RESIDENT_BF16)
    hm = _mlstm(zm, conv_w, conv_b.reshape(1, -1), gate_b, m_out_norm.reshape(1, -1),
                batch=batch, seq=seq, lc=MLSTM_CHUNK)
    ha = _moba(qkv, batch=batch, seq=seq)
    merged = _merge(h, hm, ha, w_gm, w_ga, w_up_m.astype(BF16), w_up_a.astype(BF16),
                    tm=TM_STREAMED, tn=MERGE_TN)
    x1, h1 = _outproj(x2d, merged, w_out.astype(BF16), mlp_norm.reshape(1, d), tm=TM_RESIDENT)
    mlp = _ffn(h1, w_ff1.astype(BF16), w_ff2.astype(BF16), tm=TM_STREAMED, tf=FFN_TF)
    return _ple(x1, mlp, p2d, ple_norm.reshape(1, d), w_ple_gate.astype(BF16), w_ple_proj.astype(BF16),
                final_norm.reshape(1, d), tm=TM_RESIDENT, final=final)


def kernel(x, p, positions, attn_norm, w_in, b_if, conv_w, conv_b, m_out_norm, w_up_m, w_up_a,
           w_out, mlp_norm, w_ff1, w_ff2, ple_norm, w_ple_gate, w_ple_proj, final_norm):
    batch, seq, d = x.shape
    depth = w_in.shape[0]
    t = batch * seq
    x2d = x.reshape(t, d)
    pos2d = positions.reshape(t, 1)
    for i in range(depth):
        x2d = _layer(
            x2d, p[i].reshape(t, PLE_DIM), pos2d, batch, seq, final_norm, i == depth - 1,
            attn_norm[i], w_in[i], b_if[i], conv_w[i], conv_b[i], m_out_norm[i], w_up_m[i],
            w_up_a[i], w_out[i], mlp_norm[i], w_ff1[i], w_ff2[i], ple_norm[i], w_ple_gate[i],
            w_ple_proj[i])
    return x2d.reshape(batch, seq, d)
```

```python
import functools

import jax
import jax.numpy as jnp
from jax import lax
from jax.experimental import pallas as pl
from jax.experimental.pallas import tpu as pltpu

D_MODEL = 2048
PLE_DIM = 256
M_HEADS = 4
M_QK_DIM = 128
M_V_DIM = 256
M_CONV = 4
A_HEADS = 8
A_HEAD_DIM = 128
MOBA_BLOCK = 256
MOBA_TOPK = 3
ROPE_THETA = 500000.0
ROPE_DIM = A_HEAD_DIM // 4
D_FF = 4 * D_MODEL
EPS = 1e-6

M_QK_W = M_HEADS * M_QK_DIM
M_V_W = M_HEADS * M_V_DIM
A_W = A_HEADS * A_HEAD_DIM

LANES = 128
SUBLANES = 8
BF16_SUBLANES = 16
GATE_W = LANES
ZM_W = 2 * M_QK_W + 2 * M_V_W + GATE_W
VMEM_LIMIT = 56 * 1024 * 1024

TM_RESIDENT = 512
TM_RESIDENT_BF16 = 1024
TM_STREAMED = 1024
MERGE_TN = 1024
FFN_TF = 1024
MLSTM_CHUNK = 256

F32 = jnp.float32
BF16 = jnp.bfloat16
NEG_INF = float("-inf")
LOG2_E = 1.4426950408889634


def _cparams(*sem):
    return pltpu.CompilerParams(dimension_semantics=sem, vmem_limit_bytes=VMEM_LIMIT)


def _rmsnorm(x, g):
    ms = jnp.mean(x * x, axis=-1, keepdims=True)
    return (x * lax.rsqrt(ms + EPS)) * g


def _sigmoid(x):
    return 1.0 / (1.0 + jnp.exp(-x))


def _dot(a, b):
    return jnp.dot(a, b, preferred_element_type=F32)


def _dot_nt(a, b):
    return lax.dot_general(a, b, (((1,), (1,)), ((), ())), preferred_element_type=F32)


def _resident(shape):
    return pl.BlockSpec(shape, lambda *_: (0,) * len(shape), pipeline_mode=pl.Buffered(1))


def _inproj_m_kernel(x_ref, g_ref, w_ref, wg_ref, o_ref, h_ref):
    h = _rmsnorm(x_ref[...], g_ref[...]).astype(BF16)
    h_ref[...] = h
    n = w_ref.shape[1]
    o_ref[:, 0:n] = _dot(h, w_ref[...])
    o_ref[:, n:] = _dot(h, wg_ref[...])


def _inproj_m(x, g, w, w_gate, *, tm):
    t, d = x.shape
    n = w.shape[1]
    ng = w_gate.shape[1]
    tm = min(tm, t)
    return pl.pallas_call(
        _inproj_m_kernel,
        out_shape=(jax.ShapeDtypeStruct((t, n + ng), F32), jax.ShapeDtypeStruct((t, d), BF16)),
        grid=(t // tm,),
        in_specs=[
            pl.BlockSpec((tm, d), lambda i: (i, 0)),
            _resident((1, d)),
            _resident((d, n)),
            _resident((d, ng)),
        ],
        out_specs=(pl.BlockSpec((tm, n + ng), lambda i: (i, 0)),
                   pl.BlockSpec((tm, d), lambda i: (i, 0))),
        compiler_params=_cparams("arbitrary"),
        name="inproj_m",
    )(x, g, w, w_gate)


def _inproj_a_kernel(h_ref, w_ref, pos_ref, freq_ref, o_ref):
    h = h_ref[...]
    ang = pos_ref[...].astype(F32) * freq_ref[...]
    c = jnp.cos(ang)
    s = jnp.sin(ang)
    lane = lax.broadcasted_iota(jnp.int32, c.shape, 1)
    half = ROPE_DIM // 2
    for part in range(2):
        z = _dot(h, w_ref[:, part * A_W:(part + 1) * A_W])
        for hd in range(A_HEADS):
            zh = z[:, hd * A_HEAD_DIM:(hd + 1) * A_HEAD_DIM]
            partner = jnp.where(lane < half,
                                pltpu.roll(zh, A_HEAD_DIM - half, 1),
                                pltpu.roll(zh, half, 1))
            rot = jnp.where(lane < ROPE_DIM, zh * c + partner * s, zh)
            c0 = part * A_W + hd * A_HEAD_DIM
            o_ref[:, c0:c0 + A_HEAD_DIM] = rot.astype(o_ref.dtype)
    o_ref[:, 2 * A_W:] = _dot(h, w_ref[:, 2 * A_W:]).astype(o_ref.dtype)


def _inproj_a(h, w, pos, freq, *, tm):
    t, d = h.shape
    n = w.shape[1]
    tm = min(tm, t)
    return pl.pallas_call(
        _inproj_a_kernel,
        out_shape=jax.ShapeDtypeStruct((t, n), BF16),
        grid=(t // tm,),
        in_specs=[
            pl.BlockSpec((tm, d), lambda i: (i, 0)),
            _resident((d, n)),
            pl.BlockSpec((tm, 1), lambda i: (i, 0)),
            _resident((1, A_HEAD_DIM)),
        ],
        out_specs=pl.BlockSpec((tm, n), lambda i: (i, 0)),
        compiler_params=_cparams("arbitrary"),
        name="inproj_a",
    )(h, w, pos, freq)


def _mlstm_kernel(qk_ref, v_ref, o_ref, gate_ref, cw_ref, cb_ref, gb_ref, gn_ref, out_ref,
                  buf_ref, c_ref, n_ref, m_ref, *, lc):
    ci = pl.program_id(1)
    pad = SUBLANES

    @pl.when(ci == 0)
    def _():
        buf_ref[0:pad, :] = jnp.zeros((pad, 2 * M_QK_W), F32)
        c_ref[...] = jnp.zeros(c_ref.shape, F32)
        n_ref[...] = jnp.zeros(n_ref.shape, F32)
        m_ref[...] = jnp.zeros(m_ref.shape, F32)

    def conv(window):
        y = cb_ref[...] + cw_ref[M_CONV - 1:M_CONV, :] * window(0)
        for j in range(1, M_CONV):
            y = y + cw_ref[M_CONV - 1 - j:M_CONV - j, :] * window(j)
        return y

    buf_ref[pad:2 * pad, :] = qk_ref[0:pad, :]
    y_head = conv(lambda j: buf_ref[pad - j:2 * pad - j, :])
    y_body = conv(lambda j: qk_ref[pad - j:lc - j, :])
    buf_ref[0:pad, :] = qk_ref[lc - pad:lc, :]
    y = jnp.concatenate([y_head, y_body], axis=0)
    qk = y * _sigmoid(y)

    gates = gate_ref[...] + gb_ref[...]
    lf = (jnp.minimum(gates, 0.0) - jnp.log1p(jnp.exp(-jnp.abs(gates)))) * LOG2_E
    gates = gates * LOG2_E
    row = lax.broadcasted_iota(jnp.int32, (lc, lc), 0)
    col = lax.broadcasted_iota(jnp.int32, (lc, lc), 1)
    causal = row >= col
    tri = causal.astype(F32)
    bcum = jnp.dot(tri, lf, preferred_element_type=F32, precision=lax.Precision.HIGHEST)
    gates_t = gates.T
    bcum_t = bcum.T

    heads = range(M_HEADS)
    q_bf, k_f32, k_bf, v_bf, qk_dot, qc_dot, qn_sum = [], [], [], [], [], [], []
    for hd in heads:
        q = qk[:, hd * M_QK_DIM:(hd + 1) * M_QK_DIM]
        k = qk[:, M_QK_W + hd * M_QK_DIM:M_QK_W + (hd + 1) * M_QK_DIM] * (M_QK_DIM ** -0.5)
        q_bf.append(q.astype(BF16))
        k_f32.append(k)
        k_bf.append(k.astype(BF16))
        v_bf.append(v_ref[:, hd * M_V_DIM:(hd + 1) * M_V_DIM].astype(BF16))
        qk_dot.append(_dot_nt(q_bf[hd], k_bf[hd]))
        qc_dot.append(_dot(q_bf[hd], c_ref[hd].astype(BF16)))
        qn_sum.append(jnp.sum(q * n_ref[hd], axis=-1, keepdims=True))

    for hd in heads:
        ig_col = gates[:, hd:hd + 1]
        ig_row = gates_t[hd:hd + 1, :]
        b_col = bcum[:, M_HEADS + hd:M_HEADS + hd + 1]
        b_row = bcum_t[M_HEADS + hd:M_HEADS + hd + 1, :]
        m_prev = m_ref[hd][:, 0:1]

        dmat = jnp.where(causal, b_col - b_row + ig_row, NEG_INF)
        inter = b_col + m_prev
        m_t = jnp.maximum(inter, jnp.max(dmat, axis=-1, keepdims=True))
        s = qk_dot[hd] * jnp.exp2(dmat - m_t)
        e_inter = jnp.exp2(inter - m_t)
        num = e_inter * qc_dot[hd] + _dot(s.astype(BF16), v_bf[hd])
        den = e_inter * qn_sum[hd] + jnp.sum(s, axis=-1, keepdims=True)
        h = num / jnp.maximum(jnp.abs(den), jnp.exp2(-m_t))

        hn = h * lax.rsqrt(jnp.mean(h * h, axis=-1, keepdims=True) + EPS)
        hn = hn * gn_ref[:, hd * M_V_DIM:(hd + 1) * M_V_DIM]
        og = _sigmoid(o_ref[:, hd * M_V_DIM:(hd + 1) * M_V_DIM])
        out_ref[:, hd * M_V_DIM:(hd + 1) * M_V_DIM] = (hn * og).astype(out_ref.dtype)

    for hd in heads:
        ig_col = gates[:, hd:hd + 1]
        b_col = bcum[:, M_HEADS + hd:M_HEADS + hd + 1]
        m_prev = m_ref[hd][:, 0:1]
        g_last = b_col[lc - 1:lc, :]
        a_col = g_last - b_col + ig_col
        m_new = jnp.maximum(g_last + m_prev, jnp.max(a_col, axis=0, keepdims=True))
        w_col = jnp.exp2(a_col - m_new)
        decay = jnp.exp2(g_last + m_prev - m_new)
        kw = k_f32[hd] * w_col
        c_ref[hd] = decay * c_ref[hd] + _dot(kw.T.astype(BF16), v_bf[hd])
        n_ref[hd] = decay * n_ref[hd] + jnp.sum(kw, axis=0, keepdims=True)
        m_ref[hd] = jnp.broadcast_to(m_new, (1, LANES))


def _mlstm(zm, conv_w, conv_b, gate_b, out_norm, *, batch, seq, lc):
    t = zm.shape[0]
    assert seq % lc == 0 and lc % SUBLANES == 0
    nc = seq // lc
    qkw = 2 * M_QK_W
    return pl.pallas_call(
        functools.partial(_mlstm_kernel, lc=lc),
        out_shape=jax.ShapeDtypeStruct((t, M_V_W), BF16),
        grid=(batch, nc),
        in_specs=[
            pl.BlockSpec((lc, qkw), lambda b, c: (b * nc + c, 0)),
            pl.BlockSpec((lc, M_V_W), lambda b, c: (b * nc + c, qkw // M_V_W)),
            pl.BlockSpec((lc, M_V_W), lambda b, c: (b * nc + c, qkw // M_V_W + 1)),
            pl.BlockSpec((lc, GATE_W), lambda b, c: (b * nc + c, (qkw + 2 * M_V_W) // GATE_W)),
            pl.BlockSpec((M_CONV, qkw), lambda b, c: (0, 0)),
            pl.BlockSpec((1, qkw), lambda b, c: (0, 0)),
            pl.BlockSpec((1, GATE_W), lambda b, c: (0, 0)),
            pl.BlockSpec((1, M_V_W), lambda b, c: (0, 0)),
        ],
        out_specs=pl.BlockSpec((lc, M_V_W), lambda b, c: (b * nc + c, 0)),
        scratch_shapes=[
            pltpu.VMEM((2 * SUBLANES, qkw), F32),
            pltpu.VMEM((M_HEADS, M_QK_DIM, M_V_DIM), F32),
            pltpu.VMEM((M_HEADS, 1, M_QK_DIM), F32),
            pltpu.VMEM((M_HEADS, 1, LANES), F32),
        ],
        compiler_params=_cparams("arbitrary", "arbitrary"),
        name="mlstm",
    )(zm, zm, zm, zm, conv_w, conv_b, gate_b, out_norm)


MOBA_GROUP = 4
MOBA_HEADS = 4
MASK_BIG = 1e30
VT_ROWS = A_HEAD_DIM + BF16_SUBLANES


def _moba_kernel(q_ref, k_ref, v_ref, o_ref, kaug_ref, vt_ref, kmean_ref, ksplit_ref, qaug_ref,
                 m_ref, mx_ref, alpha_ref, acc_ref, s_ref, p_ref, *, nblk):
    qi = pl.program_id(2)
    blk = MOBA_BLOCK
    hd = A_HEAD_DIM
    grp = MOBA_GROUP
    nh = MOBA_HEADS
    heads = range(nh)
    c = (A_HEAD_DIM ** -0.5) * LOG2_E

    @pl.when(qi == 0)
    def _():
        lane = lax.broadcasted_iota(jnp.int32, (blk, hd), 1)
        row = lax.broadcasted_iota(jnp.int32, (VT_ROWS - hd, grp * blk), 0)
        ones_rows = (row == 0).astype(BF16)

        def prep(g, carry):
            for h in heads:
                vt_ref[h, g, hd:VT_ROWS, :] = ones_rows
            for gg in range(grp):
                j = g * grp + gg
                r0 = pl.multiple_of(j * blk, blk)
                for h in heads:
                    kj = k_ref[pl.ds(r0, blk), h * hd:(h + 1) * hd]
                    kaug_ref[h, pl.ds(r0, blk), 0:hd] = kj
                    kaug_ref[h, pl.ds(r0, blk), hd:2 * hd] = (lane == nblk * h + j).astype(BF16)
                    kmean_ref[h, pl.ds(j, 1), :] = jnp.mean(kj.astype(F32), axis=0, keepdims=True)
                    vt_ref[h, g, 0:hd, gg * blk:(gg + 1) * blk] = (
                        v_ref[pl.ds(r0, blk), h * hd:(h + 1) * hd].astype(F32).T.astype(BF16))
            return carry
        lax.fori_loop(0, nblk // grp, prep, 0)
        for h in heads:
            rest = kmean_ref[h]
            for part in range(3):
                term = rest.astype(BF16)
                ksplit_ref[h, part] = term
                rest = rest - term.astype(F32)

    blk_id = lax.broadcasted_iota(jnp.int32, (nblk, blk), 0)
    blk_f = blk_id.astype(F32)
    past = blk_id < qi

    def scores(h):
        q = q_ref[:, h * hd:(h + 1) * hd]
        return (_dot_nt(ksplit_ref[h, 0], q) + _dot_nt(ksplit_ref[h, 1], q)) + _dot_nt(ksplit_ref[h, 2], q)

    def select(h, sc):
        q = q_ref[:, h * hd:(h + 1) * hd]
        sc = jnp.where(past, sc, NEG_INF)
        bias = jnp.where(blk_id == qi, 0.0, -MASK_BIG)
        for _ in range(MOBA_TOPK):
            mx = jnp.max(sc, axis=0, keepdims=True)
            first = jnp.min(jnp.where(sc == mx, blk_f, float(nblk)), axis=0, keepdims=True)
            first = jnp.where(mx > NEG_INF, first, -1.0)
            pick = blk_f == first
            bias = jnp.where(pick, 0.0, bias)
            sc = jnp.where(pick, NEG_INF, sc)
        pieces = []
        if h > 0:
            pieces.append(jnp.zeros((h * nblk, blk), F32))
        pieces.append(bias)
        if (h + 1) * nblk < hd:
            pieces.append(jnp.zeros((hd - (h + 1) * nblk, blk), F32))
        qaug_ref[h, :, 0:hd] = q
        qaug_ref[h, :, hd:2 * hd] = jnp.concatenate(pieces, axis=0).T.astype(BF16)

    def logits(g, h):
        c0 = pl.multiple_of(g * (grp * blk), grp * blk)
        return _dot_nt(kaug_ref[h, pl.ds(c0, grp * blk), :], qaug_ref[h])

    def softmax(h, own=None):
        m_old = m_ref[h]
        full = grp if own is None else own
        gmax = None
        for gg in range(full):
            gmax = mx_ref[h, gg] if gmax is None else jnp.maximum(gmax, mx_ref[h, gg])
        if own is not None:
            s_own = s_ref[h, own * blk:(own + 1) * blk, :]
            kpos = lax.broadcasted_iota(jnp.int32, s_own.shape, 0)
            qpos = lax.broadcasted_iota(jnp.int32, s_own.shape, 1)
            s_own = jnp.where(kpos <= qpos, s_own, -MASK_BIG)
            own_max = jnp.max(s_own, axis=0, keepdims=True)
            gmax = own_max if gmax is None else jnp.maximum(gmax, own_max)
        m_new = jnp.maximum(m_old, gmax)
        alpha_ref[h] = jnp.exp2((m_old - m_new) * c)
        if full:
            p_ref[h, 0:full * blk, :] = jnp.exp2(((s_ref[h, 0:full * blk, :] - m_new) * c).astype(BF16))
        if own is not None:
            p_ref[h, own * blk:(own + 1) * blk, :] = jnp.exp2(((s_own - m_new) * c).astype(BF16))
        m_ref[h] = m_new

    def accumulate(g, h, keys=grp * blk):
        acc_ref[h] = alpha_ref[h] * acc_ref[h] + _dot(vt_ref[h, g, :, 0:keys], p_ref[h, 0:keys, :])

    m_ref[...] = jnp.full(m_ref.shape, -MASK_BIG, F32)
    acc_ref[...] = jnp.zeros(acc_ref.shape, F32)

    last = qi // grp
    sc_all = [scores(h) for h in heads]
    def stash(h, sg):
        s_ref[h] = sg
        for gg in range(grp):
            mx_ref[h, gg] = jnp.max(sg[gg * blk:(gg + 1) * blk, :], axis=0, keepdims=True)

    for h in heads:
        select(h, sc_all[h])
        stash(h, logits(0, h))

    def body(g, carry):
        nxt = [logits(g + 1, h) for h in heads]
        for h in heads:
            softmax(h)
            accumulate(g, h)
        for h in heads:
            stash(h, nxt[h])
        return carry
    lax.fori_loop(0, last, body, 0)
    for own in range(grp):
        @pl.when(qi % grp == own)
        def _():
            for h in heads:
                softmax(h, own=own)
                accumulate(last, h, keys=(own + 1) * blk)

    for h in heads:
        acc = acc_ref[h]
        o_ref[:, h * hd:(h + 1) * hd] = (acc[0:hd] / acc[hd:hd + 1]).T.astype(o_ref.dtype)


def _moba(qkv, *, batch, seq):
    t = qkv.shape[0]
    nblk = seq // MOBA_BLOCK
    hd = A_HEAD_DIM
    nh = MOBA_HEADS
    hw = nh * hd
    assert nblk % MOBA_GROUP == 0 and nh * nblk <= hd and A_HEADS % nh == 0
    ngrp = A_HEADS // nh
    return pl.pallas_call(
        functools.partial(_moba_kernel, nblk=nblk),
        out_shape=jax.ShapeDtypeStruct((t, A_W), BF16),
        grid=(batch, ngrp, nblk),
        in_specs=[
            pl.BlockSpec((MOBA_BLOCK, hw), lambda b, h, i: (b * nblk + i, h)),
            pl.BlockSpec((seq, hw), lambda b, h, i: (b, ngrp + h), pipeline_mode=pl.Buffered(1)),
            pl.BlockSpec((seq, hw), lambda b, h, i: (b, 2 * ngrp + h), pipeline_mode=pl.Buffered(1)),
        ],
        out_specs=pl.BlockSpec((MOBA_BLOCK, hw), lambda b, h, i: (b * nblk + i, h)),
        scratch_shapes=[
            pltpu.VMEM((nh, seq, 2 * hd), BF16),
            pltpu.VMEM((nh, nblk // MOBA_GROUP, VT_ROWS, MOBA_GROUP * MOBA_BLOCK), BF16),
            pltpu.VMEM((nh, nblk, hd), F32),
            pltpu.VMEM((nh, 3, nblk, hd), BF16),
            pltpu.VMEM((nh, MOBA_BLOCK, 2 * hd), BF16),
            pltpu.VMEM((nh, 1, MOBA_BLOCK), F32),
            pltpu.VMEM((nh, MOBA_GROUP, 1, MOBA_BLOCK), F32),
            pltpu.VMEM((nh, 1, MOBA_BLOCK), F32),
            pltpu.VMEM((nh, VT_ROWS, MOBA_BLOCK), F32),
            pltpu.VMEM((nh, MOBA_GROUP * MOBA_BLOCK, MOBA_BLOCK), F32),
            pltpu.VMEM((nh, MOBA_GROUP * MOBA_BLOCK, MOBA_BLOCK), BF16),
        ],
        compiler_params=_cparams("arbitrary", "arbitrary", "arbitrary"),
        name="moba",
    )(qkv, qkv, qkv)


def _merge_kernel(h_ref, hm_ref, ha_ref, wgm_ref, wga_ref, wm_ref, wa_ref, o_ref):
    h = h_ref[...]
    gm = _sigmoid(_dot(h, wgm_ref[...]))
    ga = _sigmoid(_dot(h, wga_ref[...]))
    um = _dot(hm_ref[...], wm_ref[...])
    ua = _dot(ha_ref[...], wa_ref[...])
    o_ref[...] = (gm * um + ga * ua).astype(o_ref.dtype)


def _merge(h, hm, ha, wgm, wga, wm, wa, *, tm, tn):
    t, d = h.shape
    tm = min(tm, t)
    return pl.pallas_call(
        _merge_kernel,
        out_shape=jax.ShapeDtypeStruct((t, d), BF16),
        grid=(t // tm, d // tn),
        in_specs=[
            pl.BlockSpec((tm, d), lambda i, j: (i, 0)),
            pl.BlockSpec((tm, M_V_W), lambda i, j: (i, 0)),
            pl.BlockSpec((tm, A_W), lambda i, j: (i, 0)),
            pl.BlockSpec((d, tn), lambda i, j: (0, j)),
            pl.BlockSpec((d, tn), lambda i, j: (0, j)),
            pl.BlockSpec((M_V_W, tn), lambda i, j: (0, j)),
            pl.BlockSpec((A_W, tn), lambda i, j: (0, j)),
        ],
        out_specs=pl.BlockSpec((tm, tn), lambda i, j: (i, j)),
        compiler_params=_cparams("arbitrary", "arbitrary"),
        name="merge",
    )(h, hm, ha, wgm, wga, wm, wa)


def _outproj_kernel(x_ref, a_ref, w_ref, g_ref, o_ref, h_ref):
    y = x_ref[...] + _dot(a_ref[...], w_ref[...])
    o_ref[...] = y
    h_ref[...] = _rmsnorm(y, g_ref[...]).astype(BF16)


def _outproj(x, a, w, g, *, tm):
    t, d = x.shape
    k = a.shape[1]
    tm = min(tm, t)
    return pl.pallas_call(
        _outproj_kernel,
        out_shape=(jax.ShapeDtypeStruct((t, d), F32), jax.ShapeDtypeStruct((t, d), BF16)),
        grid=(t // tm,),
        in_specs=[
            pl.BlockSpec((tm, d), lambda i: (i, 0)),
            pl.BlockSpec((tm, k), lambda i: (i, 0)),
            _resident((k, d)),
            _resident((1, d)),
        ],
        out_specs=(pl.BlockSpec((tm, d), lambda i: (i, 0)),
                   pl.BlockSpec((tm, d), lambda i: (i, 0))),
        compiler_params=_cparams("arbitrary"),
        name="outproj",
    )(x, a, w, g)


def _ffn_kernel(h_ref, w1_ref, w2_ref, o_ref):
    def mlp():
        u = jnp.maximum(_dot(h_ref[...], w1_ref[...]), 0.0)
        return _dot((u * u).astype(BF16), w2_ref[...])

    @pl.when(pl.program_id(1) == 0)
    def _():
        o_ref[...] = mlp()

    @pl.when(pl.program_id(1) != 0)
    def _():
        o_ref[...] += mlp()


def _ffn(h, w1, w2, *, tm, tf):
    t, d = h.shape
    dff = w1.shape[1]
    tm = min(tm, t)
    return pl.pallas_call(
        _ffn_kernel,
        out_shape=jax.ShapeDtypeStruct((t, d), F32),
        grid=(t // tm, dff // tf),
        in_specs=[
            pl.BlockSpec((tm, d), lambda i, f: (i, 0)),
            pl.BlockSpec((d, tf), lambda i, f: (0, f)),
            pl.BlockSpec((tf, d), lambda i, f: (f, 0)),
        ],
        out_specs=pl.BlockSpec((tm, d), lambda i, f: (i, 0)),
        compiler_params=_cparams("arbitrary", "arbitrary"),
        name="ffn",
    )(h, w1, w2)


def _ple_kernel(x_ref, f_ref, p_ref, g_ref, wg_ref, wp_ref, gf_ref, o_ref, *, final):
    x = x_ref[...] + f_ref[...]
    h = _rmsnorm(x, g_ref[...]).astype(BF16)
    gate = _sigmoid(_dot(h, wg_ref[...]))
    emb = _dot(p_ref[...].astype(BF16), wp_ref[...])
    y = x + gate * emb
    o_ref[...] = _rmsnorm(y, gf_ref[...]) if final else y


def _ple(x, f, p, g, wg, wp, gf, *, tm, final):
    t, d = x.shape
    tm = min(tm, t)
    return pl.pallas_call(
        functools.partial(_ple_kernel, final=final),
        out_shape=jax.ShapeDtypeStruct((t, d), F32),
        grid=(t // tm,),
        in_specs=[
            pl.BlockSpec((tm, d), lambda i: (i, 0)),
            pl.BlockSpec((tm, d), lambda i: (i, 0)),
            pl.BlockSpec((tm, PLE_DIM), lambda i: (i, 0)),
            _resident((1, d)),
            _resident((d, d)),
            _resident((PLE_DIM, d)),
            _resident((1, d)),
        ],
        out_specs=pl.BlockSpec((tm, d), lambda i: (i, 0)),
        compiler_params=_cparams("arbitrary"),
        name="ple",
    )(x, f, p, g, wg, wp, gf)


def _layer(x2d, p2d, pos2d, batch, seq, final_norm, final, attn_norm, w_in, b_if, conv_w, conv_b,
           m_out_norm, w_up_m, w_up_a, w_out, mlp_norm, w_ff1, w_ff2, ple_norm, w_ple_gate,
           w_ple_proj):
    d = D_MODEL
    o_if = 2 * M_QK_W + 2 * M_V_W
    o_a = o_if + 2 * M_HEADS
    o_g = o_a + 3 * A_W
    w_gate = jnp.pad(w_in[:, o_if:o_a], ((0, 0), (0, GATE_W - 2 * M_HEADS))).astype(BF16)
    w_m = w_in[:, :o_if].astype(BF16)
    w_a = w_in[:, o_a:o_g].astype(BF16)
    w_gm = w_in[:, o_g:o_g + d].astype(BF16)
    w_ga = w_in[:, o_g + d:].astype(BF16)
    gate_b = jnp.pad(b_if.reshape(1, 2 * M_HEADS), ((0, 0), (0, GATE_W - 2 * M_HEADS)))
    g_attn = attn_norm.reshape(1, d)

    half = ROPE_DIM // 2
    inv_freq = ROPE_THETA ** (-jnp.arange(half, dtype=F32) * 2.0 / ROPE_DIM)
    freq = jnp.concatenate([-inv_freq, inv_freq, jnp.zeros((A_HEAD_DIM - ROPE_DIM,), F32)]).reshape(1, -1)

    zm, h = _inproj_m(x2d, g_attn, w_m, w_gate, tm=TM_RESIDENT)
    qkv = _inproj_a(h, w_a, pos2d, freq, tm=TM_RESIDENT_BF16)
    hm = _mlstm(zm, conv_w, conv_b.reshape(1, -1), gate_b, m_out_norm.reshape(1, -1),
                batch=batch, seq=seq, lc=MLSTM_CHUNK)
    ha = _moba(qkv, batch=batch, seq=seq)
    merged = _merge(h, hm, ha, w_gm, w_ga, w_up_m.astype(BF16), w_up_a.astype(BF16),
                    tm=TM_STREAMED, tn=MERGE_TN)
    x1, h1 = _outproj(x2d, merged, w_out.astype(BF16), mlp_norm.reshape(1, d), tm=TM_RESIDENT)
    mlp = _ffn(h1, w_ff1.astype(BF16), w_ff2.astype(BF16), tm=TM_STREAMED, tf=FFN_TF)
    return _ple(x1, mlp, p2d, ple_norm.reshape(1, d), w_ple_gate.astype(BF16), w_ple_proj.astype(BF16),
                final_norm.reshape(1, d), tm=TM_RESIDENT, final=final)


def kernel(x, p, positions, attn_norm, w_in, b_if, conv_w, conv_b, m_out_norm, w_up_m, w_up_a,
           w_out, mlp_norm, w_ff1, w_ff2, ple_norm, w_ple_gate, w_ple_proj, final_norm):
    batch, seq, d = x.shape
    depth = w_in.shape[0]
    t = batch * seq
    x2d = x.reshape(t, d)
    pos2d = positions.reshape(t, 1)
    for i in range(depth):
        x2d = _layer(
            x2d, p[i].reshape(t, PLE_DIM), pos2d, batch, seq, final_norm, i == depth - 1,
            attn_norm[i], w_in[i], b_if[i], conv_w[i], conv_b[i], m_out_norm[i], w_up_m[i],
            w_up_a[i], w_out[i], mlp_norm[i], w_ff1[i], w_ff2[i], ple_norm[i], w_ple_gate[i],
            w_ple_proj[i])
    return x2d.reshape(batch, seq, d)
```

```python
import functools

import jax
import jax.numpy as jnp
from jax import lax
from jax.experimental import pallas as pl
from jax.experimental.pallas import tpu as pltpu

D_MODEL = 2048
PLE_DIM = 256
M_HEADS = 4
M_QK_DIM = 128
M_V_DIM = 256
M_CONV = 4
A_HEADS = 8
A_HEAD_DIM = 128
MOBA_BLOCK = 256
MOBA_TOPK = 3
ROPE_THETA = 500000.0
ROPE_DIM = A_HEAD_DIM // 4
D_FF = 4 * D_MODEL
EPS = 1e-6

M_QK_W = M_HEADS * M_QK_DIM
M_V_W = M_HEADS * M_V_DIM
A_W = A_HEADS * A_HEAD_DIM

LANES = 128
SUBLANES = 8
BF16_SUBLANES = 16
GATE_W = LANES
ZM_W = 2 * M_QK_W + 2 * M_V_W + GATE_W
VMEM_LIMIT = 56 * 1024 * 1024

TM_RESIDENT = 512
TM_RESIDENT_BF16 = 1024
TM_STREAMED = 1024
MERGE_TN = 1024
FFN_TF = 1024
MLSTM_CHUNK = 256

F32 = jnp.float32
BF16 = jnp.bfloat16
NEG_INF = float("-inf")
LOG2_E = 1.4426950408889634


def _cparams(*sem):
    return pltpu.CompilerParams(dimension_semantics=sem, vmem_limit_bytes=VMEM_LIMIT)


def _rmsnorm(x, g):
    ms = jnp.mean(x * x, axis=-1, keepdims=True)
    return (x * lax.rsqrt(ms + EPS)) * g


def _sigmoid(x):
    return 1.0 / (1.0 + jnp.exp(-x))


def _dot(a, b):
    return jnp.dot(a, b, preferred_element_type=F32)


def _dot_nt(a, b):
    return lax.dot_general(a, b, (((1,), (1,)), ((), ())), preferred_element_type=F32)


def _resident(shape):
    return pl.BlockSpec(shape, lambda *_: (0,) * len(shape), pipeline_mode=pl.Buffered(1))


def _inproj_m_kernel(x_ref, g_ref, w_ref, wg_ref, o_ref, h_ref):
    h = _rmsnorm(x_ref[...], g_ref[...]).astype(BF16)
    h_ref[...] = h
    n = w_ref.shape[1]
    o_ref[:, 0:n] = _dot(h, w_ref[...])
    o_ref[:, n:] = _dot(h, wg_ref[...])


def _inproj_m(x, g, w, w_gate, *, tm):
    t, d = x.shape
    n = w.shape[1]
    ng = w_gate.shape[1]
    tm = min(tm, t)
    return pl.pallas_call(
        _inproj_m_kernel,
        out_shape=(jax.ShapeDtypeStruct((t, n + ng), F32), jax.ShapeDtypeStruct((t, d), BF16)),
        grid=(t // tm,),
        in_specs=[
            pl.BlockSpec((tm, d), lambda i: (i, 0)),
            _resident((1, d)),
            _resident((d, n)),
            _resident((d, ng)),
        ],
        out_specs=(pl.BlockSpec((tm, n + ng), lambda i: (i, 0)),
                   pl.BlockSpec((tm, d), lambda i: (i, 0))),
        compiler_params=_cparams("arbitrary"),
        name="inproj_m",
    )(x, g, w, w_gate)


def _inproj_a_kernel(h_ref, w_ref, pos_ref, freq_ref, o_ref):
    h = h_ref[...]
    ang = pos_ref[...].astype(F32) * freq_ref[...]
    c = jnp.cos(ang)
    s = jnp.sin(ang)
    lane = lax.broadcasted_iota(jnp.int32, c.shape, 1)
    half = ROPE_DIM // 2
    for part in range(2):
        z = _dot(h, w_ref[:, part * A_W:(part + 1) * A_W])
        for hd in range(A_HEADS):
            zh = z[:, hd * A_HEAD_DIM:(hd + 1) * A_HEAD_DIM]
            partner = jnp.where(lane < half,
                                pltpu.roll(zh, A_HEAD_DIM - half, 1),
                                pltpu.roll(zh, half, 1))
            rot = jnp.where(lane < ROPE_DIM, zh * c + partner * s, zh)
            c0 = part * A_W + hd * A_HEAD_DIM
            o_ref[:, c0:c0 + A_HEAD_DIM] = rot.astype(o_ref.dtype)
    o_ref[:, 2 * A_W:] = _dot(h, w_ref[:, 2 * A_W:]).astype(o_ref.dtype)


def _inproj_a(h, w, pos, freq, *, tm):
    t, d = h.shape
    n = w.shape[1]
    tm = min(tm, t)
    return pl.pallas_call(
        _inproj_a_kernel,
        out_shape=jax.ShapeDtypeStruct((t, n), BF16),
        grid=(t // tm,),
        in_specs=[
            pl.BlockSpec((tm, d), lambda i: (i, 0)),
            _resident((d, n)),
            pl.BlockSpec((tm, 1), lambda i: (i, 0)),
            _resident((1, A_HEAD_DIM)),
        ],
        out_specs=pl.BlockSpec((tm, n), lambda i: (i, 0)),
        compiler_params=_cparams("arbitrary"),
        name="inproj_a",
    )(h, w, pos, freq)


def _mlstm_kernel(qk_ref, v_ref, o_ref, gate_ref, cw_ref, cb_ref, gb_ref, gn_ref, out_ref,
                  buf_ref, c_ref, n_ref, m_ref, *, lc):
    ci = pl.program_id(1)
    pad = SUBLANES

    @pl.when(ci == 0)
    def _():
        buf_ref[0:pad, :] = jnp.zeros((pad, 2 * M_QK_W), F32)
        c_ref[...] = jnp.zeros(c_ref.shape, F32)
        n_ref[...] = jnp.zeros(n_ref.shape, F32)
        m_ref[...] = jnp.zeros(m_ref.shape, F32)

    def conv(window):
        y = cb_ref[...] + cw_ref[M_CONV - 1:M_CONV, :] * window(0)
        for j in range(1, M_CONV):
            y = y + cw_ref[M_CONV - 1 - j:M_CONV - j, :] * window(j)
        return y

    buf_ref[pad:2 * pad, :] = qk_ref[0:pad, :]
    y_head = conv(lambda j: buf_ref[pad - j:2 * pad - j, :])
    y_body = conv(lambda j: qk_ref[pad - j:lc - j, :])
    buf_ref[0:pad, :] = qk_ref[lc - pad:lc, :]
    y = jnp.concatenate([y_head, y_body], axis=0)
    qk = y * _sigmoid(y)

    gates = gate_ref[...] + gb_ref[...]
    lf = (jnp.minimum(gates, 0.0) - jnp.log1p(jnp.exp(-jnp.abs(gates)))) * LOG2_E
    gates = gates * LOG2_E
    row = lax.broadcasted_iota(jnp.int32, (lc, lc), 0)
    col = lax.broadcasted_iota(jnp.int32, (lc, lc), 1)
    causal = row >= col
    tri = causal.astype(F32)
    bcum = jnp.dot(tri, lf, preferred_element_type=F32, precision=lax.Precision.HIGHEST)
    gates_t = gates.T
    bcum_t = bcum.T

    heads = range(M_HEADS)
    q_bf, k_f32, k_bf, v_bf, qk_dot, qc_dot, qn_sum = [], [], [], [], [], [], []
    for hd in heads:
        q = qk[:, hd * M_QK_DIM:(hd + 1) * M_QK_DIM]
        k = qk[:, M_QK_W + hd * M_QK_DIM:M_QK_W + (hd + 1) * M_QK_DIM] * (M_QK_DIM ** -0.5)
        q_bf.append(q.astype(BF16))
        k_f32.append(k)
        k_bf.append(k.astype(BF16))
        v_bf.append(v_ref[:, hd * M_V_DIM:(hd + 1) * M_V_DIM].astype(BF16))
        qk_dot.append(_dot_nt(q_bf[hd], k_bf[hd]))
        qc_dot.append(_dot(q_bf[hd], c_ref[hd].astype(BF16)))
        qn_sum.append(jnp.sum(q * n_ref[hd], axis=-1, keepdims=True))

    for hd in heads:
        ig_col = gates[:, hd:hd + 1]
        ig_row = gates_t[hd:hd + 1, :]
        b_col = bcum[:, M_HEADS + hd:M_HEADS + hd + 1]
        b_row = bcum_t[M_HEADS + hd:M_HEADS + hd + 1, :]
        m_prev = m_ref[hd][:, 0:1]

        dmat = jnp.where(causal, b_col - b_row + ig_row, NEG_INF)
        inter = b_col + m_prev
        m_t = jnp.maximum(inter, jnp.max(dmat, axis=-1, keepdims=True))
        s = qk_dot[hd] * jnp.exp2(dmat - m_t)
        e_inter = jnp.exp2(inter - m_t)
        num = e_inter * qc_dot[hd] + _dot(s.astype(BF16), v_bf[hd])
        den = e_inter * qn_sum[hd] + jnp.sum(s, axis=-1, keepdims=True)
        h = num / jnp.maximum(jnp.abs(den), jnp.exp2(-m_t))

        hn = h * lax.rsqrt(jnp.mean(h * h, axis=-1, keepdims=True) + EPS)
        hn = hn * gn_ref[:, hd * M_V_DIM:(hd + 1) * M_V_DIM]
        og = _sigmoid(o_ref[:, hd * M_V_DIM:(hd + 1) * M_V_DIM])
        out_ref[:, hd * M_V_DIM:(hd + 1) * M_V_DIM] = (hn * og).astype(out_ref.dtype)

    for hd in heads:
        ig_col = gates[:, hd:hd + 1]
        b_col = bcum[:, M_HEADS + hd:M_HEADS + hd + 1]
        m_prev = m_ref[hd][:, 0:1]
        g_last = b_col[lc - 1:lc, :]
        a_col = g_last - b_col + ig_col
        m_new = jnp.maximum(g_last + m_prev, jnp.max(a_col, axis=0, keepdims=True))
        w_col = jnp.exp2(a_col - m_new)
        decay = jnp.exp2(g_last + m_prev - m_new)
        kw = k_f32[hd] * w_col
        c_ref[hd] = decay * c_ref[hd] + _dot(kw.T.astype(BF16), v_bf[hd])
        n_ref[hd] = decay * n_ref[hd] + jnp.sum(kw, axis=0, keepdims=True)
        m_ref[hd] = jnp.broadcast_to(m_new, (1, LANES))


def _mlstm(zm, conv_w, conv_b, gate_b, out_norm, *, batch, seq, lc):
    t = zm.shape[0]
    assert seq % lc == 0 and lc % SUBLANES == 0
    nc = seq // lc
    qkw = 2 * M_QK_W
    return pl.pallas_call(
        functools.partial(_mlstm_kernel, lc=lc),
        out_shape=jax.ShapeDtypeStruct((t, M_V_W), BF16),
        grid=(batch, nc),
        in_specs=[
            pl.BlockSpec((lc, qkw), lambda b, c: (b * nc + c, 0)),
            pl.BlockSpec((lc, M_V_W), lambda b, c: (b * nc + c, qkw // M_V_W)),
            pl.BlockSpec((lc, M_V_W), lambda b, c: (b * nc + c, qkw // M_V_W + 1)),
            pl.BlockSpec((lc, GATE_W), lambda b, c: (b * nc + c, (qkw + 2 * M_V_W) // GATE_W)),
            pl.BlockSpec((M_CONV, qkw), lambda b, c: (0, 0)),
            pl.BlockSpec((1, qkw), lambda b, c: (0, 0)),
            pl.BlockSpec((1, GATE_W), lambda b, c: (0, 0)),
            pl.BlockSpec((1, M_V_W), lambda b, c: (0, 0)),
        ],
        out_specs=pl.BlockSpec((lc, M_V_W), lambda b, c: (b * nc + c, 0)),
        scratch_shapes=[
            pltpu.VMEM((2 * SUBLANES, qkw), F32),
            pltpu.VMEM((M_HEADS, M_QK_DIM, M_V_DIM), F32),
            pltpu.VMEM((M_HEADS, 1, M_QK_DIM), F32),
            pltpu.VMEM((M_HEADS, 1, LANES), F32),
        ],
        compiler_params=_cparams("arbitrary", "arbitrary"),
        name="mlstm",
    )(zm, zm, zm, zm, conv_w, conv_b, gate_b, out_norm)


MOBA_GROUP = 4
MOBA_HEADS = 4
MOBA_QBLOCKS = 2
MASK_BIG = 1e30
VT_ROWS = A_HEAD_DIM + BF16_SUBLANES


def _moba_prologue(k_ref, v_ref, kaug_ref, vt_ref, kmean_ref, ksplit_ref, *, nblk):
    blk = MOBA_BLOCK
    hd = A_HEAD_DIM
    grp = MOBA_GROUP
    heads = range(MOBA_HEADS)

    @pl.when(pl.program_id(2) == 0)
    def _():
        lane = lax.broadcasted_iota(jnp.int32, (blk, hd), 1)
        row = lax.broadcasted_iota(jnp.int32, (VT_ROWS - hd, grp * blk), 0)
        ones_rows = (row == 0).astype(BF16)

        def prep(g, carry):
            for h in heads:
                vt_ref[h, g, hd:VT_ROWS, :] = ones_rows
            for gg in range(grp):
                j = g * grp + gg
                r0 = pl.multiple_of(j * blk, blk)
                for h in heads:
                    kj = k_ref[pl.ds(r0, blk), h * hd:(h + 1) * hd]
                    kaug_ref[h, pl.ds(r0, blk), 0:hd] = kj
                    kaug_ref[h, pl.ds(r0, blk), hd:2 * hd] = (lane == nblk * h + j).astype(BF16)
                    kmean_ref[h, pl.ds(j, 1), :] = jnp.mean(kj.astype(F32), axis=0, keepdims=True)
                    vt_ref[h, g, 0:hd, gg * blk:(gg + 1) * blk] = (
                        v_ref[pl.ds(r0, blk), h * hd:(h + 1) * hd].astype(F32).T.astype(BF16))
            return carry
        lax.fori_loop(0, nblk // grp, prep, 0)
        for h in heads:
            rest = kmean_ref[h]
            for part in range(3):
                term = rest.astype(BF16)
                ksplit_ref[h, part] = term
                rest = rest - term.astype(F32)


def _moba_qblock(q_ref, o_ref, kaug_ref, vt_ref, ksplit_ref, qaug_ref, m_ref, mx_ref, alpha_ref,
                 acc_ref, s_ref, p_ref, *, nblk, qi, rows):
    blk = MOBA_BLOCK
    hd = A_HEAD_DIM
    grp = MOBA_GROUP
    heads = range(MOBA_HEADS)
    c = (A_HEAD_DIM ** -0.5) * LOG2_E

    blk_id = lax.broadcasted_iota(jnp.int32, (nblk, blk), 0)
    blk_f = blk_id.astype(F32)
    past = blk_id < qi

    def scores(h):
        q = q_ref[rows, h * hd:(h + 1) * hd]
        return (_dot_nt(ksplit_ref[h, 0], q) + _dot_nt(ksplit_ref[h, 1], q)) + _dot_nt(ksplit_ref[h, 2], q)

    def select(h, sc):
        q = q_ref[rows, h * hd:(h + 1) * hd]
        sc = jnp.where(past, sc, NEG_INF)
        bias = jnp.where(blk_id == qi, 0.0, -MASK_BIG)
        for _ in range(MOBA_TOPK):
            mx = jnp.max(sc, axis=0, keepdims=True)
            first = jnp.min(jnp.where(sc == mx, blk_f, float(nblk)), axis=0, keepdims=True)
            first = jnp.where(mx > NEG_INF, first, -1.0)
            pick = blk_f == first
            bias = jnp.where(pick, 0.0, bias)
            sc = jnp.where(pick, NEG_INF, sc)
        pieces = []
        if h > 0:
            pieces.append(jnp.zeros((h * nblk, blk), F32))
        pieces.append(bias)
        if (h + 1) * nblk < hd:
            pieces.append(jnp.zeros((hd - (h + 1) * nblk, blk), F32))
        qaug_ref[h, :, 0:hd] = q
        qaug_ref[h, :, hd:2 * hd] = jnp.concatenate(pieces, axis=0).T.astype(BF16)

    def logits(g, h):
        c0 = pl.multiple_of(g * (grp * blk), grp * blk)
        return _dot_nt(kaug_ref[h, pl.ds(c0, grp * blk), :], qaug_ref[h])

    def softmax(h, own=None):
        m_old = m_ref[h]
        full = grp if own is None else own
        gmax = None
        for gg in range(full):
            gmax = mx_ref[h, gg] if gmax is None else jnp.maximum(gmax, mx_ref[h, gg])
        if own is not None:
            s_own = s_ref[h, own * blk:(own + 1) * blk, :]
            kpos = lax.broadcasted_iota(jnp.int32, s_own.shape, 0)
            qpos = lax.broadcasted_iota(jnp.int32, s_own.shape, 1)
            s_own = jnp.where(kpos <= qpos, s_own, -MASK_BIG)
            own_max = jnp.max(s_own, axis=0, keepdims=True)
            gmax = own_max if gmax is None else jnp.maximum(gmax, own_max)
        m_new = jnp.maximum(m_old, gmax)
        alpha_ref[h] = jnp.exp2((m_old - m_new) * c)
        if full:
            p_ref[h, 0:full * blk, :] = jnp.exp2(((s_ref[h, 0:full * blk, :] - m_new) * c).astype(BF16))
        if own is not None:
            p_ref[h, own * blk:(own + 1) * blk, :] = jnp.exp2(((s_own - m_new) * c).astype(BF16))
        m_ref[h] = m_new

    def accumulate(g, h, keys=grp * blk):
        acc_ref[h] = alpha_ref[h] * acc_ref[h] + _dot(vt_ref[h, g, :, 0:keys], p_ref[h, 0:keys, :])

    m_ref[...] = jnp.full(m_ref.shape, -MASK_BIG, F32)
    acc_ref[...] = jnp.zeros(acc_ref.shape, F32)

    last = qi // grp
    sc_all = [scores(h) for h in heads]
    def stash(h, sg):
        s_ref[h] = sg
        for gg in range(grp):
            mx_ref[h, gg] = jnp.max(sg[gg * blk:(gg + 1) * blk, :], axis=0, keepdims=True)

    for h in heads:
        select(h, sc_all[h])
        stash(h, logits(0, h))

    def body(g, carry):
        nxt = [logits(g + 1, h) for h in heads]
        for h in heads:
            softmax(h)
            accumulate(g, h)
        for h in heads:
            stash(h, nxt[h])
        return carry
    lax.fori_loop(0, last, body, 0)
    for own in range(grp):
        @pl.when(qi % grp == own)
        def _():
            for h in heads:
                softmax(h, own=own)
                accumulate(last, h, keys=(own + 1) * blk)

    for h in heads:
        acc = acc_ref[h]
        o_ref[rows, h * hd:(h + 1) * hd] = (acc[0:hd] / acc[hd:hd + 1]).T.astype(o_ref.dtype)


def _moba_kernel(q_ref, k_ref, v_ref, o_ref, kaug_ref, vt_ref, kmean_ref, ksplit_ref, qaug_ref,
                 m_ref, mx_ref, alpha_ref, acc_ref, s_ref, p_ref, *, nblk):
    _moba_prologue(k_ref, v_ref, kaug_ref, vt_ref, kmean_ref, ksplit_ref, nblk=nblk)
    for r in range(MOBA_QBLOCKS):
        _moba_qblock(q_ref, o_ref, kaug_ref, vt_ref, ksplit_ref, qaug_ref, m_ref, mx_ref, alpha_ref,
                     acc_ref, s_ref, p_ref, nblk=nblk, qi=pl.program_id(2) * MOBA_QBLOCKS + r,
                     rows=slice(r * MOBA_BLOCK, (r + 1) * MOBA_BLOCK))


def _moba(qkv, *, batch, seq):
    t = qkv.shape[0]
    nblk = seq // MOBA_BLOCK
    hd = A_HEAD_DIM
    nh = MOBA_HEADS
    hw = nh * hd
    assert nblk % MOBA_GROUP == 0 and nh * nblk <= hd and A_HEADS % nh == 0
    ngrp = A_HEADS // nh
    assert nblk % MOBA_QBLOCKS == 0
    nstep = nblk // MOBA_QBLOCKS
    qrows = MOBA_QBLOCKS * MOBA_BLOCK
    return pl.pallas_call(
        functools.partial(_moba_kernel, nblk=nblk),
        out_shape=jax.ShapeDtypeStruct((t, A_W), BF16),
        grid=(batch, ngrp, nstep),
        in_specs=[
            pl.BlockSpec((qrows, hw), lambda b, h, i: (b * nstep + i, h)),
            pl.BlockSpec((seq, hw), lambda b, h, i: (b, ngrp + h), pipeline_mode=pl.Buffered(1)),
            pl.BlockSpec((seq, hw), lambda b, h, i: (b, 2 * ngrp + h), pipeline_mode=pl.Buffered(1)),
        ],
        out_specs=pl.BlockSpec((qrows, hw), lambda b, h, i: (b * nstep + i, h)),
        scratch_shapes=[
            pltpu.VMEM((nh, seq, 2 * hd), BF16),
            pltpu.VMEM((nh, nblk // MOBA_GROUP, VT_ROWS, MOBA_GROUP * MOBA_BLOCK), BF16),
            pltpu.VMEM((nh, nblk, hd), F32),
            pltpu.VMEM((nh, 3, nblk, hd), BF16),
            pltpu.VMEM((nh, MOBA_BLOCK, 2 * hd), BF16),
            pltpu.VMEM((nh, 1, MOBA_BLOCK), F32),
            pltpu.VMEM((nh, MOBA_GROUP, 1, MOBA_BLOCK), F32),
            pltpu.VMEM((nh, 1, MOBA_BLOCK), F32),
            pltpu.VMEM((nh, VT_ROWS, MOBA_BLOCK), F32),
            pltpu.VMEM((nh, MOBA_GROUP * MOBA_BLOCK, MOBA_BLOCK), F32),
            pltpu.VMEM((nh, MOBA_GROUP * MOBA_BLOCK, MOBA_BLOCK), BF16),
        ],
        compiler_params=_cparams("arbitrary", "arbitrary", "arbitrary"),
        name="moba",
    )(qkv, qkv, qkv)


def _merge_kernel(h_ref, hm_ref, ha_ref, wgm_ref, wga_ref, wm_ref, wa_ref, o_ref):
    h = h_ref[...]
    gm = _sigmoid(_dot(h, wgm_ref[...]))
    ga = _sigmoid(_dot(h, wga_ref[...]))
    um = _dot(hm_ref[...], wm_ref[...])
    ua = _dot(ha_ref[...], wa_ref[...])
    o_ref[...] = (gm * um + ga * ua).astype(o_ref.dtype)


def _merge(h, hm, ha, wgm, wga, wm, wa, *, tm, tn):
    t, d = h.shape
    tm = min(tm, t)
    return pl.pallas_call(
        _merge_kernel,
        out_shape=jax.ShapeDtypeStruct((t, d), BF16),
        grid=(t // tm, d // tn),
        in_specs=[
            pl.BlockSpec((tm, d), lambda i, j: (i, 0)),
            pl.BlockSpec((tm, M_V_W), lambda i, j: (i, 0)),
            pl.BlockSpec((tm, A_W), lambda i, j: (i, 0)),
            pl.BlockSpec((d, tn), lambda i, j: (0, j)),
            pl.BlockSpec((d, tn), lambda i, j: (0, j)),
            pl.BlockSpec((M_V_W, tn), lambda i, j: (0, j)),
            pl.BlockSpec((A_W, tn), lambda i, j: (0, j)),
        ],
        out_specs=pl.BlockSpec((tm, tn), lambda i, j: (i, j)),
        compiler_params=_cparams("arbitrary", "arbitrary"),
        name="merge",
    )(h, hm, ha, wgm, wga, wm, wa)


def _outproj_kernel(x_ref, a_ref, w_ref, g_ref, o_ref, h_ref):
    y = x_ref[...] + _dot(a_ref[...], w_ref[...])
    o_ref[...] = y
    h_ref[...] = _rmsnorm(y, g_ref[...]).astype(BF16)


def _outproj(x, a, w, g, *, tm):
    t, d = x.shape
    k = a.shape[1]
    tm = min(tm, t)
    return pl.pallas_call(
        _outproj_kernel,
        out_shape=(jax.ShapeDtypeStruct((t, d), F32), jax.ShapeDtypeStruct((t, d), BF16)),
        grid=(t // tm,),
        in_specs=[
            pl.BlockSpec((tm, d), lambda i: (i, 0)),
            pl.BlockSpec((tm, k), lambda i: (i, 0)),
            _resident((k, d)),
            _resident((1, d)),
        ],
        out_specs=(pl.BlockSpec((tm, d), lambda i: (i, 0)),
                   pl.BlockSpec((tm, d), lambda i: (i, 0))),
        compiler_params=_cparams("arbitrary"),
        name="outproj",
    )(x, a, w, g)


def _ffn_kernel(h_ref, w1_ref, w2_ref, o_ref):
    def mlp():
        u = jnp.maximum(_dot(h_ref[...], w1_ref[...]), 0.0)
        return _dot((u * u).astype(BF16), w2_ref[...])

    @pl.when(pl.program_id(1) == 0)
    def _():
        o_ref[...] = mlp()

    @pl.when(pl.program_id(1) != 0)
    def _():
        o_ref[...] += mlp()


def _ffn(h, w1, w2, *, tm, tf):
    t, d = h.shape
    dff = w1.shape[1]
    tm = min(tm, t)
    return pl.pallas_call(
        _ffn_kernel,
        out_shape=jax.ShapeDtypeStruct((t, d), F32),
        grid=(t // tm, dff // tf),
        in_specs=[
            pl.BlockSpec((tm, d), lambda i, f: (i, 0)),
            pl.BlockSpec((d, tf), lambda i, f: (0, f)),
            pl.BlockSpec((tf, d), lambda i, f: (f, 0)),
        ],
        out_specs=pl.BlockSpec((tm, d), lambda i, f: (i, 0)),
        compiler_params=_cparams("arbitrary", "arbitrary"),
        name="ffn",
    )(h, w1, w2)


def _ple_kernel(x_ref, f_ref, p_ref, g_ref, wg_ref, wp_ref, gf_ref, o_ref, *, final):
    x = x_ref[...] + f_ref[...]
    h = _rmsnorm(x, g_ref[...]).astype(BF16)
    gate = _sigmoid(_dot(h, wg_ref[...]))
    emb = _dot(p_ref[...].astype(BF16), wp_ref[...])
    y = x + gate * emb
    o_ref[...] = _rmsnorm(y, gf_ref[...]) if final else y


def _ple(x, f, p, g, wg, wp, gf, *, tm, final):
    t, d = x.shape
    tm = min(tm, t)
    return pl.pallas_call(
        functools.partial(_ple_kernel, final=final),
        out_shape=jax.ShapeDtypeStruct((t, d), F32),
        grid=(t // tm,),
        in_specs=[
            pl.BlockSpec((tm, d), lambda i: (i, 0)),
            pl.BlockSpec((tm, d), lambda i: (i, 0)),
            pl.BlockSpec((tm, PLE_DIM), lambda i: (i, 0)),
            _resident((1, d)),
            _resident((d, d)),
            _resident((PLE_DIM, d)),
            _resident((1, d)),
        ],
        out_specs=pl.BlockSpec((tm, d), lambda i: (i, 0)),
        compiler_params=_cparams("arbitrary"),
        name="ple",
    )(x, f, p, g, wg, wp, gf)


def _layer(x2d, p2d, pos2d, batch, seq, final_norm, final, attn_norm, w_in, b_if, conv_w, conv_b,
           m_out_norm, w_up_m, w_up_a, w_out, mlp_norm, w_ff1, w_ff2, ple_norm, w_ple_gate,
           w_ple_proj):
    d = D_MODEL
    o_if = 2 * M_QK_W + 2 * M_V_W
    o_a = o_if + 2 * M_HEADS
    o_g = o_a + 3 * A_W
    w_gate = jnp.pad(w_in[:, o_if:o_a], ((0, 0), (0, GATE_W - 2 * M_HEADS))).astype(BF16)
    w_m = w_in[:, :o_if].astype(BF16)
    w_a = w_in[:, o_a:o_g].astype(BF16)
    w_gm = w_in[:, o_g:o_g + d].astype(BF16)
    w_ga = w_in[:, o_g + d:].astype(BF16)
    gate_b = jnp.pad(b_if.reshape(1, 2 * M_HEADS), ((0, 0), (0, GATE_W - 2 * M_HEADS)))
    g_attn = attn_norm.reshape(1, d)

    half = ROPE_DIM // 2
    inv_freq = ROPE_THETA ** (-jnp.arange(half, dtype=F32) * 2.0 / ROPE_DIM)
    freq = jnp.concatenate([-inv_freq, inv_freq, jnp.zeros((A_HEAD_DIM - ROPE_DIM,), F32)]).reshape(1, -1)

    zm, h = _inproj_m(x2d, g_attn, w_m, w_gate, tm=TM_RESIDENT)
    qkv = _inproj_a(h, w_a, pos2d, freq, tm=TM_RESIDENT_BF16)
    hm = _mlstm(zm, conv_w, conv_b.reshape(1, -1), gate_b, m_out_norm.reshape(1, -1),
                batch=batch, seq=seq, lc=MLSTM_CHUNK)
    ha = _moba(qkv, batch=batch, seq=seq)
    merged = _merge(h, hm, ha, w_gm, w_ga, w_up_m.astype(BF16), w_up_a.astype(BF16),
                    tm=TM_STREAMED, tn=MERGE_TN)
    x1, h1 = _outproj(x2d, merged, w_out.astype(BF16), mlp_norm.reshape(1, d), tm=TM_RESIDENT)
    mlp = _ffn(h1, w_ff1.astype(BF16), w_ff2.astype(BF16), tm=TM_STREAMED, tf=FFN_TF)
    return _ple(x1, mlp, p2d, ple_norm.reshape(1, d), w_ple_gate.astype(BF16), w_ple_proj.astype(BF16),
                final_norm.reshape(1, d), tm=TM_RESIDENT, final=final)


def kernel(x, p, positions, attn_norm, w_in, b_if, conv_w, conv_b, m_out_norm, w_up_m, w_up_a,
           w_out, mlp_norm, w_ff1, w_ff2, ple_norm, w_ple_gate, w_ple_proj, final_norm):
    batch, seq, d = x.shape
    depth = w_in.shape[0]
    t = batch * seq
    x2d = x.reshape(t, d)
    pos2d = positions.reshape(t, 1)
    for i in range(depth):
        x2d = _layer(
            x2d, p[i].reshape(t, PLE_DIM), pos2d, batch, seq, final_norm, i == depth - 1,
            attn_norm[i], w_in[i], b_if[i], conv_w[i], conv_b[i], m_out_norm[i], w_up_m[i],
            w_up_a[i], w_out[i], mlp_norm[i], w_ff1[i], w_ff2[i], ple_norm[i], w_ple_gate[i],
            w_ple_proj[i])
    return x2d.reshape(batch, seq, d)
```

```python
import functools

import jax
import jax.numpy as jnp
from jax import lax
from jax.experimental import pallas as pl
from jax.experimental.pallas import tpu as pltpu

D_MODEL = 2048
PLE_DIM = 256
M_HEADS = 4
M_QK_DIM = 128
M_V_DIM = 256
M_CONV = 4
A_HEADS = 8
A_HEAD_DIM = 128
MOBA_BLOCK = 256
MOBA_TOPK = 3
ROPE_THETA = 500000.0
ROPE_DIM = A_HEAD_DIM // 4
D_FF = 4 * D_MODEL
EPS = 1e-6

M_QK_W = M_HEADS * M_QK_DIM
M_V_W = M_HEADS * M_V_DIM
A_W = A_HEADS * A_HEAD_DIM

LANES = 128
SUBLANES = 8
BF16_SUBLANES = 16
GATE_W = LANES
ZM_W = 2 * M_QK_W + 2 * M_V_W + GATE_W
VMEM_LIMIT = 60 * 1024 * 1024

TM_RESIDENT = 512
TM_RESIDENT_BF16 = 1024
TM_STREAMED = 1024
MERGE_TN = 1024
FFN_TF = 1024
MLSTM_CHUNK = 256

F32 = jnp.float32
BF16 = jnp.bfloat16
NEG_INF = float("-inf")
LOG2_E = 1.4426950408889634


def _cparams(*sem):
    return pltpu.CompilerParams(dimension_semantics=sem, vmem_limit_bytes=VMEM_LIMIT)


def _rmsnorm(x, g):
    ms = jnp.mean(x * x, axis=-1, keepdims=True)
    return (x * lax.rsqrt(ms + EPS)) * g


def _sigmoid(x):
    return 1.0 / (1.0 + jnp.exp(-x))


def _dot(a, b):
    return jnp.dot(a, b, preferred_element_type=F32)


def _dot_nt(a, b):
    return lax.dot_general(a, b, (((1,), (1,)), ((), ())), preferred_element_type=F32)


def _resident(shape):
    return pl.BlockSpec(shape, lambda *_: (0,) * len(shape), pipeline_mode=pl.Buffered(1))


def _inproj_m_kernel(x_ref, g_ref, w_ref, wg_ref, o_ref, h_ref):
    h = _rmsnorm(x_ref[...], g_ref[...]).astype(BF16)
    h_ref[...] = h
    n = w_ref.shape[1]
    o_ref[:, 0:n] = _dot(h, w_ref[...])
    o_ref[:, n:] = _dot(h, wg_ref[...])


def _inproj_m(x, g, w, w_gate, *, tm):
    t, d = x.shape
    n = w.shape[1]
    ng = w_gate.shape[1]
    tm = min(tm, t)
    return pl.pallas_call(
        _inproj_m_kernel,
        out_shape=(jax.ShapeDtypeStruct((t, n + ng), F32), jax.ShapeDtypeStruct((t, d), BF16)),
        grid=(t // tm,),
        in_specs=[
            pl.BlockSpec((tm, d), lambda i: (i, 0)),
            _resident((1, d)),
            _resident((d, n)),
            _resident((d, ng)),
        ],
        out_specs=(pl.BlockSpec((tm, n + ng), lambda i: (i, 0)),
                   pl.BlockSpec((tm, d), lambda i: (i, 0))),
        compiler_params=_cparams("arbitrary"),
        name="inproj_m",
    )(x, g, w, w_gate)


def _inproj_a_kernel(h_ref, w_ref, pos_ref, freq_ref, o_ref):
    h = h_ref[...]
    ang = pos_ref[...].astype(F32) * freq_ref[...]
    c = jnp.cos(ang)
    s = jnp.sin(ang)
    lane = lax.broadcasted_iota(jnp.int32, c.shape, 1)
    half = ROPE_DIM // 2
    for part in range(2):
        z = _dot(h, w_ref[:, part * A_W:(part + 1) * A_W])
        for hd in range(A_HEADS):
            zh = z[:, hd * A_HEAD_DIM:(hd + 1) * A_HEAD_DIM]
            partner = jnp.where(lane < half,
                                pltpu.roll(zh, A_HEAD_DIM - half, 1),
                                pltpu.roll(zh, half, 1))
            rot = jnp.where(lane < ROPE_DIM, zh * c + partner * s, zh)
            c0 = part * A_W + hd * A_HEAD_DIM
            o_ref[:, c0:c0 + A_HEAD_DIM] = rot.astype(o_ref.dtype)
    o_ref[:, 2 * A_W:] = _dot(h, w_ref[:, 2 * A_W:]).astype(o_ref.dtype)


def _inproj_a(h, w, pos, freq, *, tm):
    t, d = h.shape
    n = w.shape[1]
    tm = min(tm, t)
    return pl.pallas_call(
        _inproj_a_kernel,
        out_shape=jax.ShapeDtypeStruct((t, n), BF16),
        grid=(t // tm,),
        in_specs=[
            pl.BlockSpec((tm, d), lambda i: (i, 0)),
            _resident((d, n)),
            pl.BlockSpec((tm, 1), lambda i: (i, 0)),
            _resident((1, A_HEAD_DIM)),
        ],
        out_specs=pl.BlockSpec((tm, n), lambda i: (i, 0)),
        compiler_params=_cparams("arbitrary"),
        name="inproj_a",
    )(h, w, pos, freq)


def _mlstm_kernel(qk_ref, v_ref, o_ref, gate_ref, cw_ref, cb_ref, gb_ref, gn_ref, out_ref,
                  buf_ref, c_ref, n_ref, m_ref, *, lc):
    ci = pl.program_id(1)
    pad = SUBLANES

    @pl.when(ci == 0)
    def _():
        buf_ref[0:pad, :] = jnp.zeros((pad, 2 * M_QK_W), F32)
        c_ref[...] = jnp.zeros(c_ref.shape, F32)
        n_ref[...] = jnp.zeros(n_ref.shape, F32)
        m_ref[...] = jnp.zeros(m_ref.shape, F32)

    def conv(window):
        y = cb_ref[...] + cw_ref[M_CONV - 1:M_CONV, :] * window(0)
        for j in range(1, M_CONV):
            y = y + cw_ref[M_CONV - 1 - j:M_CONV - j, :] * window(j)
        return y

    buf_ref[pad:2 * pad, :] = qk_ref[0:pad, :]
    y_head = conv(lambda j: buf_ref[pad - j:2 * pad - j, :])
    y_body = conv(lambda j: qk_ref[pad - j:lc - j, :])
    buf_ref[0:pad, :] = qk_ref[lc - pad:lc, :]
    y = jnp.concatenate([y_head, y_body], axis=0)
    qk = y * _sigmoid(y)

    gates = gate_ref[...] + gb_ref[...]
    lf = (jnp.minimum(gates, 0.0) - jnp.log1p(jnp.exp(-jnp.abs(gates)))) * LOG2_E
    gates = gates * LOG2_E
    row = lax.broadcasted_iota(jnp.int32, (lc, lc), 0)
    col = lax.broadcasted_iota(jnp.int32, (lc, lc), 1)
    causal = row >= col
    tri = causal.astype(F32)
    bcum = jnp.dot(tri, lf, preferred_element_type=F32, precision=lax.Precision.HIGHEST)
    gates_t = gates.T
    bcum_t = bcum.T

    heads = range(M_HEADS)
    q_bf, k_f32, k_bf, v_bf, qk_dot, qc_dot, qn_sum = [], [], [], [], [], [], []
    for hd in heads:
        q = qk[:, hd * M_QK_DIM:(hd + 1) * M_QK_DIM]
        k = qk[:, M_QK_W + hd * M_QK_DIM:M_QK_W + (hd + 1) * M_QK_DIM] * (M_QK_DIM ** -0.5)
        q_bf.append(q.astype(BF16))
        k_f32.append(k)
        k_bf.append(k.astype(BF16))
        v_bf.append(v_ref[:, hd * M_V_DIM:(hd + 1) * M_V_DIM].astype(BF16))
        qk_dot.append(_dot_nt(q_bf[hd], k_bf[hd]))
        qc_dot.append(_dot(q_bf[hd], c_ref[hd].astype(BF16)))
        qn_sum.append(jnp.sum(q * n_ref[hd], axis=-1, keepdims=True))

    for hd in heads:
        ig_col = gates[:, hd:hd + 1]
        ig_row = gates_t[hd:hd + 1, :]
        b_col = bcum[:, M_HEADS + hd:M_HEADS + hd + 1]
        b_row = bcum_t[M_HEADS + hd:M_HEADS + hd + 1, :]
        m_prev = m_ref[hd][:, 0:1]

        dmat = jnp.where(causal, b_col - b_row + ig_row, NEG_INF)
        inter = b_col + m_prev
        m_t = jnp.maximum(inter, jnp.max(dmat, axis=-1, keepdims=True))
        s = qk_dot[hd] * jnp.exp2(dmat - m_t)
        e_inter = jnp.exp2(inter - m_t)
        num = e_inter * qc_dot[hd] + _dot(s.astype(BF16), v_bf[hd])
        den = e_inter * qn_sum[hd] + jnp.sum(s, axis=-1, keepdims=True)
        h = num / jnp.maximum(jnp.abs(den), jnp.exp2(-m_t))

        hn = h * lax.rsqrt(jnp.mean(h * h, axis=-1, keepdims=True) + EPS)
        hn = hn * gn_ref[:, hd * M_V_DIM:(hd + 1) * M_V_DIM]
        og = _sigmoid(o_ref[:, hd * M_V_DIM:(hd + 1) * M_V_DIM])
        out_ref[:, hd * M_V_DIM:(hd + 1) * M_V_DIM] = (hn * og).astype(out_ref.dtype)

    for hd in heads:
        ig_col = gates[:, hd:hd + 1]
        b_col = bcum[:, M_HEADS + hd:M_HEADS + hd + 1]
        m_prev = m_ref[hd][:, 0:1]
        g_last = b_col[lc - 1:lc, :]
        a_col = g_last - b_col + ig_col
        m_new = jnp.maximum(g_last + m_prev, jnp.max(a_col, axis=0, keepdims=True))
        w_col = jnp.exp2(a_col - m_new)
        decay = jnp.exp2(g_last + m_prev - m_new)
        kw = k_f32[hd] * w_col
        c_ref[hd] = decay * c_ref[hd] + _dot(kw.T.astype(BF16), v_bf[hd])
        n_ref[hd] = decay * n_ref[hd] + jnp.sum(kw, axis=0, keepdims=True)
        m_ref[hd] = jnp.broadcast_to(m_new, (1, LANES))


def _mlstm(zm, conv_w, conv_b, gate_b, out_norm, *, batch, seq, lc):
    t = zm.shape[0]
    assert seq % lc == 0 and lc % SUBLANES == 0
    nc = seq // lc
    qkw = 2 * M_QK_W
    return pl.pallas_call(
        functools.partial(_mlstm_kernel, lc=lc),
        out_shape=jax.ShapeDtypeStruct((t, M_V_W), BF16),
        grid=(batch, nc),
        in_specs=[
            pl.BlockSpec((lc, qkw), lambda b, c: (b * nc + c, 0)),
            pl.BlockSpec((lc, M_V_W), lambda b, c: (b * nc + c, qkw // M_V_W)),
            pl.BlockSpec((lc, M_V_W), lambda b, c: (b * nc + c, qkw // M_V_W + 1)),
            pl.BlockSpec((lc, GATE_W), lambda b, c: (b * nc + c, (qkw + 2 * M_V_W) // GATE_W)),
            pl.BlockSpec((M_CONV, qkw), lambda b, c: (0, 0)),
            pl.BlockSpec((1, qkw), lambda b, c: (0, 0)),
            pl.BlockSpec((1, GATE_W), lambda b, c: (0, 0)),
            pl.BlockSpec((1, M_V_W), lambda b, c: (0, 0)),
        ],
        out_specs=pl.BlockSpec((lc, M_V_W), lambda b, c: (b * nc + c, 0)),
        scratch_shapes=[
            pltpu.VMEM((2 * SUBLANES, qkw), F32),
            pltpu.VMEM((M_HEADS, M_QK_DIM, M_V_DIM), F32),
            pltpu.VMEM((M_HEADS, 1, M_QK_DIM), F32),
            pltpu.VMEM((M_HEADS, 1, LANES), F32),
        ],
        compiler_params=_cparams("arbitrary", "arbitrary"),
        name="mlstm",
    )(zm, zm, zm, zm, conv_w, conv_b, gate_b, out_norm)


MOBA_GROUP = 4
MOBA_HEADS = 4
MASK_BIG = 1e30
VT_ROWS = A_HEAD_DIM + BF16_SUBLANES


def _moba_kernel(q_ref, k_ref, v_ref, o_ref, kaug_ref, vt_ref, kmean_ref, ksplit_ref, qaug_ref,
                 m_ref, mx_ref, alpha_ref, acc_ref, s_ref, p_ref, *, nblk):
    qi = pl.program_id(2)
    blk = MOBA_BLOCK
    hd = A_HEAD_DIM
    grp = MOBA_GROUP
    nh = MOBA_HEADS
    heads = range(nh)
    c = (A_HEAD_DIM ** -0.5) * LOG2_E

    @pl.when(qi == 0)
    def _():
        lane = lax.broadcasted_iota(jnp.int32, (blk, hd), 1)
        row = lax.broadcasted_iota(jnp.int32, (VT_ROWS - hd, grp * blk), 0)
        ones_rows = (row == 0).astype(BF16)

        def prep(g, carry):
            for h in heads:
                vt_ref[h, g, hd:VT_ROWS, :] = ones_rows
            for gg in range(grp):
                j = g * grp + gg
                r0 = pl.multiple_of(j * blk, blk)
                for h in heads:
                    kj = k_ref[pl.ds(r0, blk), h * hd:(h + 1) * hd]
                    kaug_ref[h, pl.ds(r0, blk), 0:hd] = kj
                    kaug_ref[h, pl.ds(r0, blk), hd:2 * hd] = (lane == nblk * h + j).astype(BF16)
                    kmean_ref[h, pl.ds(j, 1), :] = jnp.mean(kj.astype(F32), axis=0, keepdims=True)
                    vt_ref[h, g, 0:hd, gg * blk:(gg + 1) * blk] = (
                        v_ref[pl.ds(r0, blk), h * hd:(h + 1) * hd].astype(F32).T.astype(BF16))
            return carry
        lax.fori_loop(0, nblk // grp, prep, 0)
        for h in heads:
            rest = kmean_ref[h]
            for part in range(3):
                term = rest.astype(BF16)
                ksplit_ref[h, part] = term
                rest = rest - term.astype(F32)

    blk_id = lax.broadcasted_iota(jnp.int32, (nblk, blk), 0)
    blk_f = blk_id.astype(F32)
    past = blk_id < qi

    def scores(h):
        q = q_ref[:, h * hd:(h + 1) * hd]
        return (_dot_nt(ksplit_ref[h, 0], q) + _dot_nt(ksplit_ref[h, 1], q)) + _dot_nt(ksplit_ref[h, 2], q)

    def select(h, sc):
        q = q_ref[:, h * hd:(h + 1) * hd]
        sc = jnp.where(past, sc, NEG_INF)
        bias = jnp.where(blk_id == qi, 0.0, -MASK_BIG)
        for _ in range(MOBA_TOPK):
            mx = jnp.max(sc, axis=0, keepdims=True)
            first = jnp.min(jnp.where(sc == mx, blk_f, float(nblk)), axis=0, keepdims=True)
            first = jnp.where(mx > NEG_INF, first, -1.0)
            pick = blk_f == first
            bias = jnp.where(pick, 0.0, bias)
            sc = jnp.where(pick, NEG_INF, sc)
        pieces = []
        if h > 0:
            pieces.append(jnp.zeros((h * nblk, blk), F32))
        pieces.append(bias)
        if (h + 1) * nblk < hd:
            pieces.append(jnp.zeros((hd - (h + 1) * nblk, blk), F32))
        qaug_ref[h, :, 0:hd] = q
        qaug_ref[h, :, hd:2 * hd] = jnp.concatenate(pieces, axis=0).T.astype(BF16)

    def logits(g, h):
        c0 = pl.multiple_of(g * (grp * blk), grp * blk)
        return _dot_nt(kaug_ref[h, pl.ds(c0, grp * blk), :], qaug_ref[h])

    def softmax(h, own=None):
        m_old = m_ref[h]
        full = grp if own is None else own
        gmax = None
        for gg in range(full):
            gmax = mx_ref[h, gg] if gmax is None else jnp.maximum(gmax, mx_ref[h, gg])
        if own is not None:
            s_own = s_ref[h, own * blk:(own + 1) * blk, :]
            kpos = lax.broadcasted_iota(jnp.int32, s_own.shape, 0)
            qpos = lax.broadcasted_iota(jnp.int32, s_own.shape, 1)
            s_own = jnp.where(kpos <= qpos, s_own, -MASK_BIG)
            own_max = jnp.max(s_own, axis=0, keepdims=True)
            gmax = own_max if gmax is None else jnp.maximum(gmax, own_max)
        m_new = jnp.maximum(m_old, gmax)
        alpha_ref[h] = jnp.exp2((m_old - m_new) * c)
        if full:
            p_ref[h, 0:full * blk, :] = jnp.exp2(((s_ref[h, 0:full * blk, :] - m_new) * c).astype(BF16))
        if own is not None:
            p_ref[h, own * blk:(own + 1) * blk, :] = jnp.exp2(((s_own - m_new) * c).astype(BF16))
        m_ref[h] = m_new

    def accumulate(g, h, keys=grp * blk):
        acc_ref[h] = alpha_ref[h] * acc_ref[h] + _dot(vt_ref[h, g, :, 0:keys], p_ref[h, 0:keys, :])

    m_ref[...] = jnp.full(m_ref.shape, -MASK_BIG, F32)
    acc_ref[...] = jnp.zeros(acc_ref.shape, F32)

    last = qi // grp
    sc_all = [scores(h) for h in heads]
    def stash(h, sg):
        s_ref[h] = sg
        for gg in range(grp):
            mx_ref[h, gg] = jnp.max(sg[gg * blk:(gg + 1) * blk, :], axis=0, keepdims=True)

    for h in heads:
        select(h, sc_all[h])
        stash(h, logits(0, h))

    def body(g, carry):
        nxt = [logits(g + 1, h) for h in heads]
        for h in heads:
            softmax(h)
            accumulate(g, h)
        for h in heads:
            stash(h, nxt[h])
        return carry
    lax.fori_loop(0, last, body, 0)
    for own in range(grp):
        @pl.when(qi % grp == own)
        def _():
            for h in heads:
                softmax(h, own=own)
                accumulate(last, h, keys=(own + 1) * blk)

    for h in heads:
        acc = acc_ref[h]
        o_ref[:, h * hd:(h + 1) * hd] = (acc[0:hd] / acc[hd:hd + 1]).T.astype(o_ref.dtype)


def _moba(qkv, *, batch, seq):
    t = qkv.shape[0]
    nblk = seq // MOBA_BLOCK
    hd = A_HEAD_DIM
    nh = MOBA_HEADS
    hw = nh * hd
    assert nblk % MOBA_GROUP == 0 and nh * nblk <= hd and A_HEADS % nh == 0
    ngrp = A_HEADS // nh
    return pl.pallas_call(
        functools.partial(_moba_kernel, nblk=nblk),
        out_shape=jax.ShapeDtypeStruct((t, A_W), BF16),
        grid=(batch, ngrp, nblk),
        in_specs=[
            pl.BlockSpec((MOBA_BLOCK, hw), lambda b, h, i: (b * nblk + i, h)),
            pl.BlockSpec((seq, hw), lambda b, h, i: (b, ngrp + h), pipeline_mode=pl.Buffered(1)),
            pl.BlockSpec((seq, hw), lambda b, h, i: (b, 2 * ngrp + h)),
        ],
        out_specs=pl.BlockSpec((MOBA_BLOCK, hw), lambda b, h, i: (b * nblk + i, h)),
        scratch_shapes=[
            pltpu.VMEM((nh, seq, 2 * hd), BF16),
            pltpu.VMEM((nh, nblk // MOBA_GROUP, VT_ROWS, MOBA_GROUP * MOBA_BLOCK), BF16),
            pltpu.VMEM((nh, nblk, hd), F32),
            pltpu.VMEM((nh, 3, nblk, hd), BF16),
            pltpu.VMEM((nh, MOBA_BLOCK, 2 * hd), BF16),
            pltpu.VMEM((nh, 1, MOBA_BLOCK), F32),
            pltpu.VMEM((nh, MOBA_GROUP, 1, MOBA_BLOCK), F32),
            pltpu.VMEM((nh, 1, MOBA_BLOCK), F32),
            pltpu.VMEM((nh, VT_ROWS, MOBA_BLOCK), F32),
            pltpu.VMEM((nh, MOBA_GROUP * MOBA_BLOCK, MOBA_BLOCK), F32),
            pltpu.VMEM((nh, MOBA_GROUP * MOBA_BLOCK, MOBA_BLOCK), BF16),
        ],
        compiler_params=_cparams("arbitrary", "arbitrary", "arbitrary"),
        name="moba",
    )(qkv, qkv, qkv)


def _merge_kernel(h_ref, hm_ref, ha_ref, wgm_ref, wga_ref, wm_ref, wa_ref, o_ref):
    h = h_ref[...]
    gm = _sigmoid(_dot(h, wgm_ref[...]))
    ga = _sigmoid(_dot(h, wga_ref[...]))
    um = _dot(hm_ref[...], wm_ref[...])
    ua = _dot(ha_ref[...], wa_ref[...])
    o_ref[...] = (gm * um + ga * ua).astype(o_ref.dtype)


def _merge(h, hm, ha, wgm, wga, wm, wa, *, tm, tn):
    t, d = h.shape
    tm = min(tm, t)
    return pl.pallas_call(
        _merge_kernel,
        out_shape=jax.ShapeDtypeStruct((t, d), BF16),
        grid=(t // tm, d // tn),
        in_specs=[
            pl.BlockSpec((tm, d), lambda i, j: (i, 0)),
            pl.BlockSpec((tm, M_V_W), lambda i, j: (i, 0)),
            pl.BlockSpec((tm, A_W), lambda i, j: (i, 0)),
            pl.BlockSpec((d, tn), lambda i, j: (0, j)),
            pl.BlockSpec((d, tn), lambda i, j: (0, j)),
            pl.BlockSpec((M_V_W, tn), lambda i, j: (0, j)),
            pl.BlockSpec((A_W, tn), lambda i, j: (0, j)),
        ],
        out_specs=pl.BlockSpec((tm, tn), lambda i, j: (i, j)),
        compiler_params=_cparams("arbitrary", "arbitrary"),
        name="merge",
    )(h, hm, ha, wgm, wga, wm, wa)


def _outproj_kernel(x_ref, a_ref, w_ref, g_ref, o_ref, h_ref):
    y = x_ref[...] + _dot(a_ref[...], w_ref[...])
    o_ref[...] = y
    h_ref[...] = _rmsnorm(y, g_ref[...]).astype(BF16)


def _outproj(x, a, w, g, *, tm):
    t, d = x.shape
    k = a.shape[1]
    tm = min(tm, t)
    return pl.pallas_call(
        _outproj_kernel,
        out_shape=(jax.ShapeDtypeStruct((t, d), F32), jax.ShapeDtypeStruct((t, d), BF16)),
        grid=(t // tm,),
        in_specs=[
            pl.BlockSpec((tm, d), lambda i: (i, 0)),
            pl.BlockSpec((tm, k), lambda i: (i, 0)),
            _resident((k, d)),
            _resident((1, d)),
        ],
        out_specs=(pl.BlockSpec((tm, d), lambda i: (i, 0)),
                   pl.BlockSpec((tm, d), lambda i: (i, 0))),
        compiler_params=_cparams("arbitrary"),
        name="outproj",
    )(x, a, w, g)


def _ffn_kernel(h_ref, w1_ref, w2_ref, o_ref):
    def mlp():
        u = jnp.maximum(_dot(h_ref[...], w1_ref[...]), 0.0)
        return _dot((u * u).astype(BF16), w2_ref[...])

    @pl.when(pl.program_id(1) == 0)
    def _():
        o_ref[...] = mlp()

    @pl.when(pl.program_id(1) != 0)
    def _():
        o_ref[...] += mlp()


def _ffn(h, w1, w2, *, tm, tf):
    t, d = h.shape
    dff = w1.shape[1]
    tm = min(tm, t)
    return pl.pallas_call(
        _ffn_kernel,
        out_shape=jax.ShapeDtypeStruct((t, d), F32),
        grid=(t // tm, dff // tf),
        in_specs=[
            pl.BlockSpec((tm, d), lambda i, f: (i, 0)),
            pl.BlockSpec((d, tf), lambda i, f: (0, f)),
            pl.BlockSpec((tf, d), lambda i, f: (f, 0)),
        ],
        out_specs=pl.BlockSpec((tm, d), lambda i, f: (i, 0)),
        compiler_params=_cparams("arbitrary", "arbitrary"),
        name="ffn",
    )(h, w1, w2)


def _ple_kernel(x_ref, f_ref, p_ref, g_ref, wg_ref, wp_ref, gf_ref, o_ref, *, final):
    x = x_ref[...] + f_ref[...]
    h = _rmsnorm(x, g_ref[...]).astype(BF16)
    gate = _sigmoid(_dot(h, wg_ref[...]))
    emb = _dot(p_ref[...].astype(BF16), wp_ref[...])
    y = x + gate * emb
    o_ref[...] = _rmsnorm(y, gf_ref[...]) if final else y


def _ple(x, f, p, g, wg, wp, gf, *, tm, final):
    t, d = x.shape
    tm = min(tm, t)
    return pl.pallas_call(
        functools.partial(_ple_kernel, final=final),
        out_shape=jax.ShapeDtypeStruct((t, d), F32),
        grid=(t // tm,),
        in_specs=[
            pl.BlockSpec((tm, d), lambda i: (i, 0)),
            pl.BlockSpec((tm, d), lambda i: (i, 0)),
            pl.BlockSpec((tm, PLE_DIM), lambda i: (i, 0)),
            _resident((1, d)),
            _resident((d, d)),
            _resident((PLE_DIM, d)),
            _resident((1, d)),
        ],
        out_specs=pl.BlockSpec((tm, d), lambda i: (i, 0)),
        compiler_params=_cparams("arbitrary"),
        name="ple",
    )(x, f, p, g, wg, wp, gf)


def _layer(x2d, p2d, pos2d, batch, seq, final_norm, final, attn_norm, w_in, b_if, conv_w, conv_b,
           m_out_norm, w_up_m, w_up_a, w_out, mlp_norm, w_ff1, w_ff2, ple_norm, w_ple_gate,
           w_ple_proj):
    d = D_MODEL
    o_if = 2 * M_QK_W + 2 * M_V_W
    o_a = o_if + 2 * M_HEADS
    o_g = o_a + 3 * A_W
    w_gate = jnp.pad(w_in[:, o_if:o_a], ((0, 0), (0, GATE_W - 2 * M_HEADS))).astype(BF16)
    w_m = w_in[:, :o_if].astype(BF16)
    w_a = w_in[:, o_a:o_g].astype(BF16)
    w_gm = w_in[:, o_g:o_g + d].astype(BF16)
    w_ga = w_in[:, o_g + d:].astype(BF16)
    gate_b = jnp.pad(b_if.reshape(1, 2 * M_HEADS), ((0, 0), (0, GATE_W - 2 * M_HEADS)))
    g_attn = attn_norm.reshape(1, d)

    half = ROPE_DIM // 2
    inv_freq = ROPE_THETA ** (-jnp.arange(half, dtype=F32) * 2.0 / ROPE_DIM)
    freq = jnp.concatenate([-inv_freq, inv_freq, jnp.zeros((A_HEAD_DIM - ROPE_DIM,), F32)]).reshape(1, -1)

    zm, h = _inproj_m(x2d, g_attn, w_m, w_gate, tm=TM_RESIDENT)
    qkv = _inproj_a(h, w_a, pos2d, freq, tm=TM_RESIDENT_BF16)
    hm = _mlstm(zm, conv_w, conv_b.reshape(1, -1), gate_b, m_out_norm.reshape(1, -1),
                batch=batch, seq=seq, lc=MLSTM_CHUNK)
    ha = _moba(qkv, batch=batch, seq=seq)
    merged = _merge(h, hm, ha, w_gm, w_ga, w_up_m.astype(BF16), w_up_a.astype(BF16),
                    tm=TM_STREAMED, tn=MERGE_TN)
    x1, h1 = _outproj(x2d, merged, w_out.astype(BF16), mlp_norm.reshape(1, d), tm=TM_RESIDENT)
    mlp = _ffn(h1, w_ff1.astype(BF16), w_ff2.astype(BF16), tm=TM_STREAMED, tf=FFN_TF)
    return _ple(x1, mlp, p2d, ple_norm.reshape(1, d), w_ple_gate.astype(BF16), w_ple_proj.astype(BF16),
                final_norm.reshape(1, d), tm=TM_RESIDENT, final=final)


def kernel(x, p, positions, attn_norm, w_in, b_if, conv_w, conv_b, m_out_norm, w_up_m, w_up_a,
           w_out, mlp_norm, w_ff1, w_ff2, ple_norm, w_ple_gate, w_ple_proj, final_norm):
    batch, seq, d = x.shape
    depth = w_in.shape[0]
    t = batch * seq
    x2d = x.reshape(t, d)
    pos2d = positions.reshape(t, 1)
    for i in range(depth):
        x2d = _layer(
            x2d, p[i].reshape(t, PLE_DIM), pos2d, batch, seq, final_norm, i == depth - 1,
            attn_norm[i], w_in[i], b_if[i], conv_w[i], conv_b[i], m_out_norm[i], w_up_m[i],
            w_up_a[i], w_out[i], mlp_norm[i], w_ff1[i], w_ff2[i], ple_norm[i], w_ple_gate[i],
            w_ple_proj[i])
    return x2d.reshape(batch, seq, d)
```
